```python
import math
import jax, jax.numpy as jnp
from jax import lax
import numpy as np

D_MODEL = 1024
BATCH = 4
SEQ = 8192
DEPTH = 4

D_FF = 2816
EPS = 1e-6
A_HEADS = 8
A_KV_HEADS = 2
A_HEAD_DIM = 64
WINDOW = 128
B_HEADS = 4
B_KEY_DIM = 128
B_VAL_DIM = 128
B_CHUNK = 32
S5_GROUP = 16
S5_GROUPS = D_MODEL // S5_GROUP
S5_STATE = 64
S5_DT_MIN = 1e-3
S5_DT_MAX = 1e-1

A_Q_W = A_HEADS * A_HEAD_DIM
A_KV_W = A_KV_HEADS * A_HEAD_DIM
B_W = B_HEADS * B_KEY_DIM
MIX_SPLITS = (A_Q_W, A_Q_W + A_KV_W, A_Q_W + 2 * A_KV_W, A_Q_W + 2 * A_KV_W + B_W, A_Q_W + 2 * A_KV_W + 2 * B_W, A_Q_W + 2 * A_KV_W + 3 * B_W)
MIX_IN = A_Q_W + 2 * A_KV_W + 4 * B_W
MIX_OUT = A_Q_W + B_HEADS * B_VAL_DIM
N_EVEN = (DEPTH + 1) // 2
N_ODD = DEPTH // 2

kernel_name = 'hybrid_swa_hgrn2_s5_macaron'


def _rmsnorm(x, g):
    x32 = x.astype(jnp.float32)
    y = x32 * lax.rsqrt(jnp.mean(x32 * x32, axis=-1, keepdims=True) + EPS)
    return (y * g.astype(jnp.float32)).astype(x.dtype)


def _swiglu(h, w_in, w_out):
    gate, up = jnp.split(h @ w_in, 2, axis=-1)
    return (jax.nn.silu(gate) * up) @ w_out


def _alibi_slopes():
    s = 2.0 ** (-8.0 * np.arange(1, A_HEADS + 1) / A_HEADS)
    return jnp.asarray(s, dtype=jnp.float32).reshape(A_KV_HEADS, A_HEADS // A_KV_HEADS)


def _sliding_window_gqa(q, k, v, sinks):
    b, l, _ = q.shape
    nb = l // WINDOW
    grp = A_HEADS // A_KV_HEADS
    qb = q.reshape(b, nb, WINDOW, A_KV_HEADS, grp, A_HEAD_DIM)
    kb = k.reshape(b, nb, WINDOW, A_KV_HEADS, A_HEAD_DIM)
    vb = v.reshape(b, nb, WINDOW, A_KV_HEADS, A_HEAD_DIM)
    pad = ((0, 0), (1, 0), (0, 0), (0, 0), (0, 0))
    kw = jnp.concatenate([jnp.pad(kb[:, :-1], pad), kb], axis=2)
    vw = jnp.concatenate([jnp.pad(vb[:, :-1], pad), vb], axis=2)
    s = jnp.einsum('bnqhgd,bnkhd->bnhgqk', qb, kw, preferred_element_type=jnp.float32) * (A_HEAD_DIM ** -0.5)
    qpos = jnp.arange(WINDOW) + WINDOW
    kpos = jnp.arange(2 * WINDOW)
    dist_i = qpos[:, None] - kpos[None, :]
    in_band = (dist_i >= 0) & (dist_i < WINDOW)
    has_prev = (jnp.arange(nb)[:, None, None] > 0) | (kpos[None, None, :] >= WINDOW)
    mask = in_band[None] & has_prev
    dist = dist_i.astype(jnp.float32)
    s = s - _alibi_slopes()[:, :, None, None] * dist
    s = jnp.where(mask[None, :, None, None], s, -jnp.inf)
    sink = sinks.astype(jnp.float32).reshape(A_KV_HEADS, grp)[None, None, :, :, None, None]
    m = jnp.maximum(jnp.max(s, axis=-1, keepdims=True), sink)
    p = jnp.exp(s - m)
    denom = jnp.sum(p, axis=-1, keepdims=True) + jnp.exp(sink - m)
    o = jnp.einsum('bnhgqk,bnkhd->bnhgqd', p / denom, vw.astype(jnp.float32))
    o = o.transpose(0, 1, 4, 2, 3, 5).reshape(b, l, A_Q_W)
    return o.astype(q.dtype)


def _hgrn2(q, f_logit, i_val, g, lb):
    b, l, _ = q.shape
    nc = l // B_CHUNK

    def heads(t):
        return t.astype(jnp.float32).reshape(b, nc, B_CHUNK, B_HEADS, -1).transpose(0, 3, 1, 2, 4)

    lbh = lb.astype(jnp.float32).reshape(B_HEADS, 1, 1, B_KEY_DIM)
    f = lbh + (1.0 - lbh) * jax.nn.sigmoid(heads(f_logit))
    k = 1.0 - f
    qh = jax.nn.silu(heads(q))
    vh = heads(i_val)
    cum = jnp.cumsum(jnp.log(f), axis=3)
    q_dec = qh * jnp.exp(cum)
    k_inv = k * jnp.exp(-cum)
    k_end = k * jnp.exp(cum[:, :, :, -1:] - cum)
    causal = jnp.tril(jnp.ones((B_CHUNK, B_CHUNK), dtype=bool))
    scores = jnp.where(causal, jnp.einsum('bhnqd,bhnkd->bhnqk', q_dec, k_inv), 0.0)
    o = jnp.einsum('bhnqk,bhnkv->bhnqv', scores, vh)
    chunk_decay = jnp.exp(cum[:, :, :, -1])
    chunk_kv = jnp.einsum('bhnkd,bhnkv->bhndv', k_end, vh)

    def step(state, inp):
        dec, kv = inp
        return dec[..., None] * state + kv, state

    s0 = jnp.zeros((b, B_HEADS, B_KEY_DIM, B_VAL_DIM), jnp.float32)
    _, s_start = lax.scan(step, s0, (jnp.moveaxis(chunk_decay, 2, 0), jnp.moveaxis(chunk_kv, 2, 0)))
    o = o + jnp.einsum('bhnqd,nbhdv->bhnqv', q_dec, s_start)
    o = o.transpose(0, 2, 3, 1, 4).reshape(b, l, B_HEADS, B_VAL_DIM)
    o = o * lax.rsqrt(jnp.mean(o * o, axis=-1, keepdims=True) + EPS)
    gate = jax.nn.silu(g.astype(jnp.float32)).reshape(b, l, B_HEADS, B_VAL_DIM)
    return (o * gate).reshape(b, l, B_HEADS * B_VAL_DIM).astype(q.dtype)


def _complex_linear_combine(e1, e2):
    ar1, ai1, br1, bi1 = e1
    ar2, ai2, br2, bi2 = e2
    ar = ar2 * ar1 - ai2 * ai1
    ai = ar2 * ai1 + ai2 * ar1
    br = ar2 * br1 - ai2 * bi1 + br2
    bi = ar2 * bi1 + ai2 * br1 + bi2
    return (ar, ai, br, bi)


def _s5(u, a_re, a_im, log_step, b_re, b_im, c_re, c_im, d_skip):
    b, l, _ = u.shape
    u32 = u.astype(jnp.float32)
    lam_r = a_re.astype(jnp.float32)
    lam_i = a_im.astype(jnp.float32)
    step = jnp.exp(log_step.astype(jnp.float32))[:, None]
    mag = jnp.exp(step * lam_r)
    ab_r = mag * jnp.cos(step * lam_i)
    ab_i = mag * jnp.sin(step * lam_i)
    den = lam_r * lam_r + lam_i * lam_i
    coef_r = ((ab_r - 1.0) * lam_r + ab_i * lam_i) / den
    coef_i = (ab_i * lam_r - (ab_r - 1.0) * lam_i) / den
    br = b_re.astype(jnp.float32)
    bi = b_im.astype(jnp.float32)
    bb_r = coef_r[..., None] * br - coef_i[..., None] * bi
    bb_i = coef_r[..., None] * bi + coef_i[..., None] * br
    ug = u32.reshape(b, l, S5_GROUPS, S5_GROUP)
    bu_r = jnp.einsum('blgc,gpc->blgp', ug, bb_r)
    bu_i = jnp.einsum('blgc,gpc->blgp', ug, bb_i)
    a_shape = (1, l, S5_GROUPS, S5_STATE)
    ar = jnp.broadcast_to(ab_r[None, None], a_shape)
    ai = jnp.broadcast_to(ab_i[None, None], a_shape)
    _, _, xr, xi = lax.associative_scan(_complex_linear_combine, (ar, ai, bu_r, bu_i), axis=1)
    y = jnp.einsum('blgp,gcp->blgc', xr, c_re.astype(jnp.float32)) - jnp.einsum('blgp,gcp->blgc', xi, c_im.astype(jnp.float32))
    return y.reshape(b, l, D_MODEL) + d_skip.astype(jnp.float32) * u32


def setup_inputs(seed: int = 0) -> dict:
    key = jax.random.key(seed)
    ks = jax.random.split(key, 18)
    f32 = jnp.float32

    def nrm(k, shape, scale):
        return jax.random.normal(k, shape, f32) * scale

    n = jnp.arange(S5_STATE, dtype=f32)
    return {
        'x': nrm(ks[0], (BATCH, SEQ, D_MODEL), 1.0),
        'norm_g': 1.0 + nrm(ks[1], (DEPTH, 3, D_MODEL), 0.02),
        'ffn_w_in': nrm(ks[2], (DEPTH, 2, D_MODEL, 2 * D_FF), D_MODEL ** -0.5),
        'ffn_w_out': nrm(ks[3], (DEPTH, 2, D_FF, D_MODEL), D_FF ** -0.5),
        'mix_w_in': nrm(ks[4], (N_EVEN, D_MODEL, MIX_IN), D_MODEL ** -0.5),
        'attn_sinks': nrm(ks[5], (N_EVEN, A_HEADS), 0.5),
        'hgrn_lb': nrm(ks[6], (N_EVEN, B_HEADS * B_KEY_DIM), 0.1),
        'mix_w_out': nrm(ks[7], (N_EVEN, MIX_OUT, D_MODEL), MIX_OUT ** -0.5),
        's5_a_re': -0.5 + nrm(ks[8], (N_ODD, S5_GROUPS, S5_STATE), 0.01),
        's5_a_im': math.pi * n + nrm(ks[9], (N_ODD, S5_GROUPS, S5_STATE), 0.01),
        's5_log_step': jax.random.uniform(ks[10], (N_ODD, S5_GROUPS), f32, math.log(S5_DT_MIN), math.log(S5_DT_MAX)),
        's5_b_re': nrm(ks[11], (N_ODD, S5_GROUPS, S5_STATE, S5_GROUP), (2 * S5_GROUP) ** -0.5),
        's5_b_im': nrm(ks[12], (N_ODD, S5_GROUPS, S5_STATE, S5_GROUP), (2 * S5_GROUP) ** -0.5),
        's5_c_re': nrm(ks[13], (N_ODD, S5_GROUPS, S5_GROUP, S5_STATE), S5_STATE ** -0.5),
        's5_c_im': nrm(ks[14], (N_ODD, S5_GROUPS, S5_GROUP, S5_STATE), S5_STATE ** -0.5),
        's5_d': nrm(ks[15], (N_ODD, D_MODEL), 1.0),
        's5_w_glu': nrm(ks[16], (N_ODD, D_MODEL, 2 * D_MODEL), D_MODEL ** -0.5),
        'final_g': 1.0 + nrm(ks[17], (D_MODEL,), 0.02),
    }


def reference(x, norm_g, ffn_w_in, ffn_w_out, mix_w_in, attn_sinks, hgrn_lb, mix_w_out, s5_a_re, s5_a_im, s5_log_step, s5_b_re, s5_b_im, s5_c_re, s5_c_im, s5_d, s5_w_glu, final_g):
    lb_p = jax.nn.softmax(hgrn_lb.astype(jnp.float32), axis=0)
    lb_all = jnp.cumsum(lb_p, axis=0) - lb_p[0]
    for layer in range(DEPTH):
        x = x + 0.5 * _swiglu(_rmsnorm(x, norm_g[layer, 0]), ffn_w_in[layer, 0], ffn_w_out[layer, 0])
        h = _rmsnorm(x, norm_g[layer, 1])
        if layer % 2 == 0:
            e = layer // 2
            q_a, k_a, v_a, q_b, f_b, i_b, g_b = jnp.split(h @ mix_w_in[e], MIX_SPLITS, axis=-1)
            attn = _sliding_window_gqa(q_a, k_a, v_a, attn_sinks[e])
            rec = _hgrn2(q_b, f_b, i_b, g_b, lb_all[e])
            mix = jnp.concatenate([attn, rec], axis=-1) @ mix_w_out[e]
        else:
            o = layer // 2
            y = _s5(h, s5_a_re[o], s5_a_im[o], s5_log_step[o], s5_b_re[o], s5_b_im[o], s5_c_re[o], s5_c_im[o], s5_d[o])
            y = jax.nn.gelu(y).astype(x.dtype)
            val, gate = jnp.split(y @ s5_w_glu[o], 2, axis=-1)
            mix = val * jax.nn.sigmoid(gate)
        x = x + mix.astype(x.dtype)
        x = x + 0.5 * _swiglu(_rmsnorm(x, norm_g[layer, 2]), ffn_w_in[layer, 1], ffn_w_out[layer, 1])
    return _rmsnorm(x, final_g)
```

```python
import functools
import math

import jax
import jax.numpy as jnp
import numpy as np
from jax import lax
from jax.experimental import pallas as pl
from jax.experimental.pallas import tpu as pltpu

F32 = jnp.float32
BF16 = jnp.bfloat16

EPS = 1e-6
DEPTH = 4
D_MODEL = 1024
D_FF = 2816
A_HEADS = 8
A_KV_HEADS = 2
A_GROUP = A_HEADS // A_KV_HEADS
A_HEAD_DIM = 64
WINDOW = 128
A_Q_W = A_HEADS * A_HEAD_DIM
A_KV_W = A_KV_HEADS * A_HEAD_DIM
B_HEADS = 4
B_DIM = 128
B_CHUNK = 32
B_W = B_HEADS * B_DIM
MIX_IN = A_Q_W + 2 * A_KV_W + 4 * B_W
S5_GROUP = 16
S5_GROUPS = D_MODEL // S5_GROUP
S5_STATE = 64
S5_T = 16
S5_ROW = S5_T * S5_GROUP

LANES = 128
VMEM_LIMIT = 48 * 1024 * 1024


def _cparams(*sem):
    return pltpu.CompilerParams(dimension_semantics=sem, vmem_limit_bytes=VMEM_LIMIT)


def _sigmoid(x):
    return 1.0 / (1.0 + jnp.exp(-x))


def _silu(x):
    return x * _sigmoid(x)


def _rms(x, g):
    return x * lax.rsqrt(jnp.mean(x * x, axis=-1, keepdims=True) + EPS) * g


def _dot(a, b):
    return jnp.dot(a, b, preferred_element_type=F32)


def _dot_nt(a, b):
    return lax.dot_general(a, b, (((1,), (1,)), ((), ())), preferred_element_type=F32)


def _ffn_kernel(x_ref, g_ref, wg_ref, wu_ref, wo_ref, fg_ref, o_ref, h_ref, acc_ref, *, n_ff, final):
    j = pl.program_id(1)

    @pl.when(j == 0)
    def _():
        h_ref[...] = _rms(x_ref[...], g_ref[...]).astype(BF16)
        acc_ref[...] = jnp.zeros_like(acc_ref)

    h = h_ref[...]
    gate = _dot(h, wg_ref[...])
    up = _dot(h, wu_ref[...])
    act = (_silu(gate) * up).astype(BF16)
    acc_ref[...] += _dot(act, wo_ref[...])

    @pl.when(j == n_ff - 1)
    def _():
        y = x_ref[...] + 0.5 * acc_ref[...]
        if final:
            y = _rms(y, fg_ref[...])
        o_ref[...] = y


def _ffn(x, g, w_in, w_out, final_g, *, final, tm=1024, tf=256):
    n, d = x.shape
    n_ff = D_FF // tf
    return pl.pallas_call(
        functools.partial(_ffn_kernel, n_ff=n_ff, final=final),
        grid=(n // tm, n_ff),
        in_specs=[
            pl.BlockSpec((tm, d), lambda i, j: (i, 0)),
            pl.BlockSpec((1, d), lambda i, j: (0, 0)),
            pl.BlockSpec((d, tf), lambda i, j: (0, j)),
            pl.BlockSpec((d, tf), lambda i, j: (0, j + n_ff)),
            pl.BlockSpec((tf, d), lambda i, j: (j, 0)),
            pl.BlockSpec((1, d), lambda i, j: (0, 0)),
        ],
        out_specs=pl.BlockSpec((tm, d), lambda i, j: (i, 0)),
        out_shape=jax.ShapeDtypeStruct((n, d), F32),
        scratch_shapes=[pltpu.VMEM((tm, d), BF16), pltpu.VMEM((tm, d), F32)],
        compiler_params=_cparams("parallel", "arbitrary"),
        name="ffn",
    )(x, g, w_in, w_in, w_out, final_g)


def _norm_proj_kernel(x_ref, g_ref, w_ref, o_ref):
    h = _rms(x_ref[...], g_ref[...]).astype(BF16)
    o_ref[...] = _dot(h, w_ref[...])


def _norm_proj(x, g, w, *, tm=512):
    n, d = x.shape
    m = w.shape[1]
    return pl.pallas_call(
        _norm_proj_kernel,
        grid=(n // tm,),
        in_specs=[
            pl.BlockSpec((tm, d), lambda i: (i, 0)),
            pl.BlockSpec((1, d), lambda i: (0, 0)),
            pl.BlockSpec((d, m), lambda i: (0, 0)),
        ],
        out_specs=pl.BlockSpec((tm, m), lambda i: (i, 0)),
        out_shape=jax.ShapeDtypeStruct((n, m), F32),
        compiler_params=_cparams("parallel"),
        name="norm_proj",
    )(x, g, w)


def _norm_kernel(x_ref, g_ref, o_ref, ob_ref):
    h = _rms(x_ref[...], g_ref[...])
    o_ref[...] = h
    ob_ref[...] = h.astype(BF16)


def _norm(x, g, *, tm=1024):
    n, d = x.shape
    return pl.pallas_call(
        _norm_kernel,
        grid=(n // tm,),
        in_specs=[pl.BlockSpec((tm, d), lambda i: (i, 0)), pl.BlockSpec((1, d), lambda i: (0, 0))],
        out_specs=[pl.BlockSpec((tm, d), lambda i: (i, 0)), pl.BlockSpec((tm, d), lambda i: (i, 0))],
        out_shape=[jax.ShapeDtypeStruct((n, d), F32), jax.ShapeDtypeStruct((n, d), BF16)],
        compiler_params=_cparams("parallel"),
        name="norm",
    )(x, g)


def _attn_kernel(sink_ref, q_ref, kc_ref, kp_ref, vc_ref, vp_ref, o_ref):
    hk = pl.program_id(1)
    nblk = pl.program_id(2)
    w = WINDOW
    dh = A_HEAD_DIM
    qw = A_GROUP * dh

    def widen(prev_ref, cur_ref):
        t = jnp.concatenate([prev_ref[...], cur_ref[...]], axis=0)
        r = pltpu.roll(t, dh, axis=1)
        lane = lax.broadcasted_iota(jnp.int32, t.shape, 1)
        t2 = jnp.where(lane // dh == hk, t, r)
        return jnp.concatenate([t2, t2], axis=1).astype(BF16)

    k4 = widen(kp_ref, kc_ref)
    v4 = widen(vp_ref, vc_ref)

    q = q_ref[...] * (dh ** -0.5)
    lane_head = lax.broadcasted_iota(jnp.int32, (w, qw), 1) // dh
    q4 = jnp.concatenate(
        [jnp.where(lane_head == g, q, 0.0) for g in range(A_GROUP)], axis=0).astype(BF16)

    s = _dot_nt(q4, k4)

    row = lax.broadcasted_iota(jnp.int32, (A_GROUP * w, 2 * w), 0)
    kpos = lax.broadcasted_iota(jnp.int32, (A_GROUP * w, 2 * w), 1)
    dist = (row % w) + w - kpos
    mask = (dist >= 0) & (dist < w) & ((nblk > 0) | (kpos >= w))

    grp = lax.broadcasted_iota(jnp.int32, (A_GROUP * w, 1), 0) // w
    base = jnp.where(hk == 0, 0.5, 0.5 ** (A_GROUP + 1)).astype(F32)
    slope = base * jnp.where(grp == 0, 1.0, jnp.where(grp == 1, 0.5, jnp.where(grp == 2, 0.25, 0.125)))
    sink = jnp.zeros((A_GROUP * w, 1), F32)
    for g in range(A_GROUP):
        sink = jnp.where(grp == g, sink_ref[0, hk * A_GROUP + g], sink)

    s = s - slope * dist.astype(F32)
    s = jnp.where(mask, s, -jnp.inf)
    m = jnp.maximum(jnp.max(s, axis=-1, keepdims=True), sink)
    p = jnp.exp(s - m)
    denom = jnp.sum(p, axis=-1, keepdims=True) + jnp.exp(sink - m)
    o4 = _dot(p.astype(BF16), v4) / denom

    o = jnp.zeros((w, qw), F32)
    for g in range(A_GROUP):
        o = o + jnp.where(lane_head == g, o4[g * w:(g + 1) * w, :], 0.0)
    o_ref[...] = o.astype(o_ref.dtype)


def _attention(proj, sinks, batch, seq):
    n = proj.shape[0]
    nb = seq // WINDOW
    qw = A_GROUP * A_HEAD_DIM
    kcol = A_Q_W // LANES
    vcol = (A_Q_W + A_KV_W) // LANES
    cur = lambda b, h, i: b * nb + i
    prev = lambda b, h, i: b * nb + jnp.maximum(i - 1, 0)
    return pl.pallas_call(
        _attn_kernel,
        grid=(batch, A_KV_HEADS, nb),
        in_specs=[
            pl.BlockSpec(memory_space=pltpu.SMEM),
            pl.BlockSpec((WINDOW, qw), lambda b, h, i: (cur(b, h, i), h)),
            pl.BlockSpec((WINDOW, LANES), lambda b, h, i: (cur(b, h, i), kcol)),
            pl.BlockSpec((WINDOW, LANES), lambda b, h, i: (prev(b, h, i), kcol)),
            pl.BlockSpec((WINDOW, LANES), lambda b, h, i: (cur(b, h, i), vcol)),
            pl.BlockSpec((WINDOW, LANES), lambda b, h, i: (prev(b, h, i), vcol)),
        ],
        out_specs=pl.BlockSpec((WINDOW, qw), lambda b, h, i: (cur(b, h, i), h)),
        out_shape=jax.ShapeDtypeStruct((n, A_Q_W), BF16),
        compiler_params=_cparams("parallel", "parallel", "parallel"),
        name="swa",
    )(sinks.reshape(1, A_HEADS), proj, proj, proj, proj, proj)


def _split3(x):
    hi = x.astype(BF16)
    r1 = x - hi.astype(F32)
    mid = r1.astype(BF16)
    lo = (r1 - mid.astype(F32)).astype(BF16)
    return hi, mid, lo


def _hgrn_kernel(lb_ref, q_ref, f_ref, i_ref, g_ref, o_ref, st_ref, *, ts):
    c = B_CHUNK
    nchunk = ts // c

    @pl.when(pl.program_id(2) == 0)
    def _():
        st_ref[...] = jnp.zeros_like(st_ref)

    lb = lb_ref[...]
    f = lb + (1.0 - lb) * _sigmoid(f_ref[...])
    k = 1.0 - f
    logf = jnp.log(f)

    ri = lax.broadcasted_iota(jnp.int32, (ts, ts), 0)
    ci = lax.broadcasted_iota(jnp.int32, (ts, ts), 1)
    same = (ri // c) == (ci // c)
    tri = (same & (ci <= ri)).astype(BF16)
    blk = same.astype(BF16)
    parts = _split3(logf)
    cum = sum(_dot(tri, p) for p in parts)
    tot = sum(_dot(blk, p) for p in parts)

    qd = _silu(q_ref[...]) * jnp.exp(cum)
    kinv = (k * jnp.exp(-cum)).astype(BF16)
    kend = (k * jnp.exp(tot - cum)).astype(BF16)
    dec = jnp.exp(tot)
    qd16 = qd.astype(BF16)
    v32 = i_ref[...]
    v16 = v32.astype(BF16)

    sub = LANES
    r2 = lax.broadcasted_iota(jnp.int32, (sub, sub), 0)
    c2 = lax.broadcasted_iota(jnp.int32, (sub, sub), 1)
    causal = ((r2 // c) == (c2 // c)) & (c2 <= r2)

    v_t = v32.T
    per_sub = sub // c
    outs = []
    kv_ts = []
    for sidx in range(ts // sub):
        sl = slice(sidx * sub, (sidx + 1) * sub)
        sc = jnp.where(causal, _dot_nt(qd16[sl], kinv[sl]), 0.0)
        outs.append(_dot(sc.astype(BF16), v16[sl]))
        v_ts = v_t[:, sl]
        lhs = jnp.concatenate(
            [jnp.where(c2 // c == jj, v_ts, 0.0) for jj in range(per_sub)], axis=0).astype(BF16)
        kv_ts.append(_dot(lhs, kend[sl]))
    o_intra = jnp.concatenate(outs, axis=0)

    st = st_ref[...]
    inter = []
    for ch in range(nchunk):
        sl = slice(ch * c, (ch + 1) * c)
        inter.append(_dot_nt(qd16[sl], st.astype(BF16)))
        jj = ch % per_sub
        kv_t = kv_ts[ch // per_sub][jj * B_DIM:(jj + 1) * B_DIM]
        st = st * dec[ch * c:ch * c + 1, :] + kv_t
    st_ref[...] = st
    o = o_intra + jnp.concatenate(inter, axis=0)

    o = o * lax.rsqrt(jnp.mean(o * o, axis=-1, keepdims=True) + EPS)
    o_ref[...] = (o * _silu(g_ref[...])).astype(o_ref.dtype)


def _hgrn(proj, lb, batch, seq, *, ts=512):
    n = proj.shape[0]
    nt = seq // ts
    base = (A_Q_W + 2 * A_KV_W) // LANES
    nh = B_W // LANES

    def col(kind):
        return lambda b, h, t: (b * nt + t, base + kind * nh + h)

    return pl.pallas_call(
        functools.partial(_hgrn_kernel, ts=ts),
        grid=(batch, B_HEADS, nt),
        in_specs=[
            pl.BlockSpec((1, LANES), lambda b, h, t: (0, h)),
            pl.BlockSpec((ts, LANES), col(0)),
            pl.BlockSpec((ts, LANES), col(1)),
            pl.BlockSpec((ts, LANES), col(2)),
            pl.BlockSpec((ts, LANES), col(3)),
        ],
        out_specs=pl.BlockSpec((ts, LANES), lambda b, h, t: (b * nt + t, h)),
        out_shape=jax.ShapeDtypeStruct((n, B_W), BF16),
        scratch_shapes=[pltpu.VMEM((B_DIM, B_DIM), F32)],
        compiler_params=_cparams("parallel", "parallel", "arbitrary"),
        name="hgrn2",
    )(lb.reshape(1, B_W), proj, proj, proj, proj)


def _mix_out_kernel(x_ref, a_ref, b_ref, wa_ref, wb_ref, o_ref):
    o_ref[...] = x_ref[...] + _dot(a_ref[...], wa_ref[...]) + _dot(b_ref[...], wb_ref[...])


def _mix_out(x, a, b, w, *, tm=1024):
    n, d = x.shape
    ka, kb = a.shape[1], b.shape[1]
    return pl.pallas_call(
        _mix_out_kernel,
        grid=(n // tm,),
        in_specs=[
            pl.BlockSpec((tm, d), lambda i: (i, 0)),
            pl.BlockSpec((tm, ka), lambda i: (i, 0)),
            pl.BlockSpec((tm, kb), lambda i: (i, 0)),
            pl.BlockSpec((ka, d), lambda i: (0, 0)),
            pl.BlockSpec((kb, d), lambda i: (1, 0)),
        ],
        out_specs=pl.BlockSpec((tm, d), lambda i: (i, 0)),
        out_shape=jax.ShapeDtypeStruct((n, d), F32),
        compiler_params=_cparams("parallel"),
        name="mix_out",
    )(x, a, b, w, w)


def _s5_sum_kernel(u_ref, w_ref, o_ref):
    o_ref[...] = _dot(u_ref[0], w_ref[0])


def _s5_sum(u, w):
    g, m, r = u.shape
    return pl.pallas_call(
        _s5_sum_kernel,
        grid=(g,),
        in_specs=[pl.BlockSpec((1, m, r), lambda i: (i, 0, 0)),
                  pl.BlockSpec((1, r, LANES), lambda i: (i, 0, 0))],
        out_specs=pl.BlockSpec((m, LANES), lambda i: (0, i)),
        out_shape=jax.ShapeDtypeStruct((m, g * LANES), F32),
        compiler_params=_cparams("parallel"),
        name="s5_chunk_sum",
    )(u, w)


def _s5_scan_kernel(s_ref, a1_ref, a2_ref, o_ref, x_ref, *, tc):
    @pl.when(pl.program_id(0) == 0)
    def _():
        x_ref[...] = jnp.zeros_like(x_ref)

    a1 = a1_ref[...]
    a2 = a2_ref[...]

    def step(c, x):
        o_ref[c] = x
        return a1 * x + a2 * pltpu.roll(x, S5_STATE, axis=1) + s_ref[c]

    x_ref[...] = lax.fori_loop(0, tc, step, x_ref[...])


def _s5_scan(s, a1, a2, *, tc=32):
    nc, rows, lanes = s.shape
    return pl.pallas_call(
        functools.partial(_s5_scan_kernel, tc=tc),
        grid=(nc // tc,),
        in_specs=[pl.BlockSpec((tc, rows, lanes), lambda i: (i, 0, 0)),
                  pl.BlockSpec((rows, lanes), lambda i: (0, 0)),
                  pl.BlockSpec((rows, lanes), lambda i: (0, 0))],
        out_specs=pl.BlockSpec((tc, rows, lanes), lambda i: (i, 0, 0)),
        out_shape=jax.ShapeDtypeStruct(s.shape, F32),
        scratch_shapes=[pltpu.VMEM((rows, lanes), F32)],
        compiler_params=_cparams("arbitrary"),
        name="s5_chunk_scan",
    )(s, a1, a2)


def _s5_out_kernel(u_ref, x_ref, wt_ref, wc_ref, o_ref):
    o_ref[0] = _dot(u_ref[0], wt_ref[0]) + _dot(x_ref[...].astype(BF16), wc_ref[0])


def _s5_out(u, xprev, w_toep, w_carry):
    g, m, r = u.shape
    return pl.pallas_call(
        _s5_out_kernel,
        grid=(g,),
        in_specs=[pl.BlockSpec((1, m, r), lambda i: (i, 0, 0)),
                  pl.BlockSpec((m, LANES), lambda i: (0, i)),
                  pl.BlockSpec((1, r, r), lambda i: (i, 0, 0)),
                  pl.BlockSpec((1, LANES, r), lambda i: (i, 0, 0))],
        out_specs=pl.BlockSpec((1, m, r), lambda i: (i, 0, 0)),
        out_shape=jax.ShapeDtypeStruct((g, m, r), F32),
        compiler_params=_cparams("parallel"),
        name="s5_chunk_out",
    )(u, xprev, w_toep, w_carry)


def _s5_glu_kernel(x_ref, h_ref, y_ref, d_ref, w_ref, o_ref):
    y = y_ref[...] + d_ref[...] * h_ref[...]
    inner = math.sqrt(2.0 / math.pi) * (y + 0.044715 * (y * y * y))
    ge = y * (0.5 * (1.0 + jnp.tanh(inner)))
    z = _dot(ge.astype(BF16), w_ref[...])
    d = x_ref.shape[1]
    o_ref[...] = x_ref[...] + z[:, :d] * _sigmoid(z[:, d:])


def _s5_glu(x, h, y, dskip, w, *, tm=512):
    n, d = x.shape
    tile = pl.BlockSpec((tm, d), lambda i: (i, 0))
    return pl.pallas_call(
        _s5_glu_kernel,
        grid=(n // tm,),
        in_specs=[tile, tile, tile,
                  pl.BlockSpec((1, d), lambda i: (0, 0)),
                  pl.BlockSpec((d, 2 * d), lambda i: (0, 0))],
        out_specs=tile,
        out_shape=jax.ShapeDtypeStruct((n, d), F32),
        compiler_params=_cparams("parallel"),
        name="s5_glu",
    )(x, h, y, dskip, w)


def _s5_tables(a_re, a_im, log_step, b_re, b_im, c_re, c_im, batch):
    hp = lax.Precision.HIGHEST
    t = S5_T
    step = jnp.exp(log_step)[:, None]
    mag = jnp.exp(step * a_re)
    ab_r = mag * jnp.cos(step * a_im)
    ab_i = mag * jnp.sin(step * a_im)
    den = a_re * a_re + a_im * a_im
    coef_r = ((ab_r - 1.0) * a_re + ab_i * a_im) / den
    coef_i = (ab_i * a_re - (ab_r - 1.0) * a_im) / den
    bb_r = coef_r[..., None] * b_re - coef_i[..., None] * b_im
    bb_i = coef_r[..., None] * b_im + coef_i[..., None] * b_re
    j = jnp.arange(t + 1, dtype=F32)[:, None, None]
    pmag = jnp.exp(j * (step * a_re)[None])
    pw_r = pmag * jnp.cos(j * (step * a_im)[None])
    pw_i = pmag * jnp.sin(j * (step * a_im)[None])

    rev_r, rev_i = pw_r[t - 1::-1], pw_i[t - 1::-1]
    ws_r = rev_r[:, :, :, None] * bb_r[None] - rev_i[:, :, :, None] * bb_i[None]
    ws_i = rev_r[:, :, :, None] * bb_i[None] + rev_i[:, :, :, None] * bb_r[None]
    w_sum = jnp.concatenate([ws_r, ws_i], axis=2)
    w_sum = w_sum.transpose(1, 0, 3, 2).reshape(S5_GROUPS, S5_ROW, 2 * S5_STATE)

    lb_r = pw_r[:t, :, :, None] * bb_r[None] - pw_i[:t, :, :, None] * bb_i[None]
    lb_i = pw_r[:t, :, :, None] * bb_i[None] + pw_i[:t, :, :, None] * bb_r[None]
    kmat = (jnp.einsum('gdp,jgpc->jgdc', c_re, lb_r, precision=hp)
            - jnp.einsum('gdp,jgpc->jgdc', c_im, lb_i, precision=hp))
    s_idx = jnp.arange(t)[:, None]
    t_idx = jnp.arange(t)[None, :]
    lag = t_idx - s_idx
    toep = kmat[jnp.clip(lag, 0, t - 1)]
    toep = jnp.where((lag >= 0)[:, :, None, None, None], toep, 0.0)
    w_toep = toep.transpose(2, 0, 4, 1, 3).reshape(S5_GROUPS, S5_ROW, S5_ROW)

    q_r, q_i = pw_r[1:], pw_i[1:]
    wc_r = c_re[None] * q_r[:, :, None, :] - c_im[None] * q_i[:, :, None, :]
    wc_i = -(c_re[None] * q_i[:, :, None, :] + c_im[None] * q_r[:, :, None, :])
    w_carry = jnp.concatenate([wc_r, wc_i], axis=3)
    w_carry = w_carry.transpose(1, 3, 0, 2).reshape(S5_GROUPS, 2 * S5_STATE, S5_ROW)

    a1 = jnp.concatenate([pw_r[t], pw_r[t]], axis=1)
    a2 = jnp.concatenate([-pw_i[t], pw_i[t]], axis=1)
    a1 = jnp.tile(a1, (batch, 1))
    a2 = jnp.tile(a2, (batch, 1))
    return w_sum.astype(BF16), w_toep.astype(BF16), w_carry.astype(BF16), a1, a2


def _s5_mixer(x, g, tables, dskip, w_glu, batch, seq):
    w_sum, w_toep, w_carry, a1, a2 = tables
    n, d = x.shape
    nc = seq // S5_T
    h, h16 = _norm(x, g)
    u = h16.reshape(batch, nc, S5_T, S5_GROUPS, S5_GROUP).transpose(3, 1, 0, 2, 4)
    u = u.reshape(S5_GROUPS, nc * batch, S5_ROW)
    s = _s5_sum(u, w_sum)
    s3 = s.reshape(nc, batch * S5_GROUPS, 2 * S5_STATE)
    xprev = _s5_scan(s3, a1, a2).reshape(nc * batch, S5_GROUPS * 2 * S5_STATE)
    y = _s5_out(u, xprev, w_toep, w_carry)
    y = y.reshape(S5_GROUPS, nc, batch, S5_T, S5_GROUP).transpose(2, 1, 3, 0, 4).reshape(n, d)
    return _s5_glu(x, h, y, dskip, w_glu)


def kernel(x, norm_g, ffn_w_in, ffn_w_out, mix_w_in, attn_sinks, hgrn_lb, mix_w_out, s5_a_re, s5_a_im, s5_log_step, s5_b_re, s5_b_im, s5_c_re, s5_c_im, s5_d, s5_w_glu, final_g):
    batch, seq, d = x.shape
    n = batch * seq
    x = x.reshape(n, d)

    w_in16 = ffn_w_in.astype(BF16)
    w_out16 = ffn_w_out.astype(BF16)
    mix_in16 = mix_w_in.astype(BF16)
    mix_out16 = mix_w_out.astype(BF16)
    glu16 = s5_w_glu.astype(BF16)
    fg = final_g.reshape(1, d)

    lb_p = jax.nn.softmax(hgrn_lb.astype(F32), axis=0)
    lb_all = jnp.cumsum(lb_p, axis=0) - lb_p[0]

    for layer in range(DEPTH):
        g3 = norm_g[layer].reshape(3, 1, d)
        x = _ffn(x, g3[0], w_in16[layer, 0], w_out16[layer, 0], fg, final=False)
        if layer % 2 == 0:
            e = layer // 2
            proj = _norm_proj(x, g3[1], mix_in16[e])
            attn = _attention(proj, attn_sinks[e], batch, seq)
            rec = _hgrn(proj, lb_all[e], batch, seq)
            x = _mix_out(x, attn, rec, mix_out16[e])
        else:
            o = layer // 2
            tables = _s5_tables(s5_a_re[o], s5_a_im[o], s5_log_step[o], s5_b_re[o], s5_b_im[o],
                                s5_c_re[o], s5_c_im[o], batch)
            x = _s5_mixer(x, g3[1], tables, s5_d[o].reshape(1, d), glu16[o], batch, seq)
        x = _ffn(x, g3[2], w_in16[layer, 1], w_out16[layer, 1], fg, final=(layer == DEPTH - 1))
    return x.reshape(batch, seq, d)
```

```python
import functools
import math

import jax
import jax.numpy as jnp
import numpy as np
from jax import lax
from jax.experimental import pallas as pl
from jax.experimental.pallas import tpu as pltpu

F32 = jnp.float32
BF16 = jnp.bfloat16

EPS = 1e-6
DEPTH = 4
D_MODEL = 1024
D_FF = 2816
A_HEADS = 8
A_KV_HEADS = 2
A_GROUP = A_HEADS // A_KV_HEADS
A_HEAD_DIM = 64
WINDOW = 128
A_Q_W = A_HEADS * A_HEAD_DIM
A_KV_W = A_KV_HEADS * A_HEAD_DIM
B_HEADS = 4
B_DIM = 128
B_CHUNK = 32
B_W = B_HEADS * B_DIM
MIX_IN = A_Q_W + 2 * A_KV_W + 4 * B_W
S5_GROUP = 16
S5_GROUPS = D_MODEL // S5_GROUP
S5_STATE = 64
S5_T = 16
S5_ROW = S5_T * S5_GROUP

LANES = 128
VMEM_LIMIT = 48 * 1024 * 1024


def _cparams(*sem):
    return pltpu.CompilerParams(dimension_semantics=sem, vmem_limit_bytes=VMEM_LIMIT)


def _sigmoid(x):
    return 1.0 / (1.0 + jnp.exp(-x))


def _silu(x):
    return x * _sigmoid(x)


def _rms(x, g):
    return x * lax.rsqrt(jnp.mean(x * x, axis=-1, keepdims=True) + EPS) * g


def _dot(a, b):
    return jnp.dot(a, b, preferred_element_type=F32)


def _dot_nt(a, b):
    return lax.dot_general(a, b, (((1,), (1,)), ((), ())), preferred_element_type=F32)


def _ffn_kernel(x_ref, g_ref, wi_ref, wo_ref, fg_ref, o_ref, *, tf, final):
    x = x_ref[...]
    h = _rms(x, g_ref[...]).astype(BF16)
    acc = None
    for c in range(D_FF // tf):
        gate = _dot(h, wi_ref[:, c * tf:(c + 1) * tf])
        up = _dot(h, wi_ref[:, D_FF + c * tf:D_FF + (c + 1) * tf])
        act = (_silu(gate) * up).astype(BF16)
        part = _dot(act, wo_ref[c * tf:(c + 1) * tf, :])
        acc = part if acc is None else acc + part
    y = x + 0.5 * acc
    if final:
        y = _rms(y, fg_ref[...])
    o_ref[...] = y


def _resident(shape):
    return pl.BlockSpec(shape, lambda *_: (0,) * len(shape), pipeline_mode=pl.Buffered(1))


def _ffn(x, g, w_in, w_out, final_g, *, final, tm=512, tf=256):
    n, d = x.shape
    return pl.pallas_call(
        functools.partial(_ffn_kernel, tf=tf, final=final),
        grid=(n // tm,),
        in_specs=[
            pl.BlockSpec((tm, d), lambda i: (i, 0)),
            _resident((1, d)),
            _resident((d, 2 * D_FF)),
            _resident((D_FF, d)),
            _resident((1, d)),
        ],
        out_specs=pl.BlockSpec((tm, d), lambda i: (i, 0)),
        out_shape=jax.ShapeDtypeStruct((n, d), F32),
        compiler_params=_cparams("parallel"),
        name="ffn",
    )(x, g, w_in, w_out, final_g)


def _norm_proj_kernel(x_ref, g_ref, w_ref, o_ref):
    h = _rms(x_ref[...], g_ref[...]).astype(BF16)
    o_ref[...] = _dot(h, w_ref[...])


def _norm_proj(x, g, w, *, tm=512):
    n, d = x.shape
    m = w.shape[1]
    return pl.pallas_call(
        _norm_proj_kernel,
        grid=(n // tm,),
        in_specs=[
            pl.BlockSpec((tm, d), lambda i: (i, 0)),
            pl.BlockSpec((1, d), lambda i: (0, 0)),
            pl.BlockSpec((d, m), lambda i: (0, 0)),
        ],
        out_specs=pl.BlockSpec((tm, m), lambda i: (i, 0)),
        out_shape=jax.ShapeDtypeStruct((n, m), F32),
        compiler_params=_cparams("parallel"),
        name="norm_proj",
    )(x, g, w)


def _norm_kernel(x_ref, g_ref, o_ref):
    o_ref[...] = _rms(x_ref[...], g_ref[...])


def _norm(x, g, *, tm=1024):
    n, d = x.shape
    return pl.pallas_call(
        _norm_kernel,
        grid=(n // tm,),
        in_specs=[pl.BlockSpec((tm, d), lambda i: (i, 0)), pl.BlockSpec((1, d), lambda i: (0, 0))],
        out_specs=pl.BlockSpec((tm, d), lambda i: (i, 0)),
        out_shape=jax.ShapeDtypeStruct((n, d), F32),
        compiler_params=_cparams("parallel"),
        name="norm",
    )(x, g)


def _attn_kernel(sink_ref, q_ref, kc_ref, kp_ref, vc_ref, vp_ref, o_ref):
    hk = pl.program_id(1)
    nblk = pl.program_id(2)
    w = WINDOW
    dh = A_HEAD_DIM
    qw = A_GROUP * dh

    def widen(prev_ref, cur_ref):
        t = jnp.concatenate([prev_ref[...], cur_ref[...]], axis=0)
        r = pltpu.roll(t, dh, axis=1)
        lane = lax.broadcasted_iota(jnp.int32, t.shape, 1)
        t2 = jnp.where(lane // dh == hk, t, r)
        return jnp.concatenate([t2, t2], axis=1).astype(BF16)

    k4 = widen(kp_ref, kc_ref)
    v4 = widen(vp_ref, vc_ref)

    q = q_ref[...] * (dh ** -0.5)
    lane_head = lax.broadcasted_iota(jnp.int32, (w, qw), 1) // dh
    q4 = jnp.concatenate(
        [jnp.where(lane_head == g, q, 0.0) for g in range(A_GROUP)], axis=0).astype(BF16)

    s = _dot_nt(q4, k4)

    row = lax.broadcasted_iota(jnp.int32, (A_GROUP * w, 2 * w), 0)
    kpos = lax.broadcasted_iota(jnp.int32, (A_GROUP * w, 2 * w), 1)
    dist = (row % w) + w - kpos
    mask = (dist >= 0) & (dist < w) & ((nblk > 0) | (kpos >= w))

    grp = lax.broadcasted_iota(jnp.int32, (A_GROUP * w, 1), 0) // w
    base = jnp.where(hk == 0, 0.5, 0.5 ** (A_GROUP + 1)).astype(F32)
    slope = base * jnp.where(grp == 0, 1.0, jnp.where(grp == 1, 0.5, jnp.where(grp == 2, 0.25, 0.125)))
    sink = jnp.zeros((A_GROUP * w, 1), F32)
    for g in range(A_GROUP):
        sink = jnp.where(grp == g, sink_ref[0, hk * A_GROUP + g], sink)

    s = s - slope * dist.astype(F32)
    s = jnp.where(mask, s, -jnp.inf)
    m = jnp.maximum(jnp.max(s, axis=-1, keepdims=True), sink)
    p = jnp.exp(s - m)
    denom = jnp.sum(p, axis=-1, keepdims=True) + jnp.exp(sink - m)
    o4 = _dot(p.astype(BF16), v4) / denom

    o = jnp.zeros((w, qw), F32)
    for g in range(A_GROUP):
        o = o + jnp.where(lane_head == g, o4[g * w:(g + 1) * w, :], 0.0)
    o_ref[...] = o.astype(o_ref.dtype)


def _attention(proj, sinks, batch, seq):
    n = proj.shape[0]
    nb = seq // WINDOW
    qw = A_GROUP * A_HEAD_DIM
    kcol = A_Q_W // LANES
    vcol = (A_Q_W + A_KV_W) // LANES
    cur = lambda b, h, i: b * nb + i
    prev = lambda b, h, i: b * nb + jnp.maximum(i - 1, 0)
    return pl.pallas_call(
        _attn_kernel,
        grid=(batch, A_KV_HEADS, nb),
        in_specs=[
            pl.BlockSpec(memory_space=pltpu.SMEM),
            pl.BlockSpec((WINDOW, qw), lambda b, h, i: (cur(b, h, i), h)),
            pl.BlockSpec((WINDOW, LANES), lambda b, h, i: (cur(b, h, i), kcol)),
            pl.BlockSpec((WINDOW, LANES), lambda b, h, i: (prev(b, h, i), kcol)),
            pl.BlockSpec((WINDOW, LANES), lambda b, h, i: (cur(b, h, i), vcol)),
            pl.BlockSpec((WINDOW, LANES), lambda b, h, i: (prev(b, h, i), vcol)),
        ],
        out_specs=pl.BlockSpec((WINDOW, qw), lambda b, h, i: (cur(b, h, i), h)),
        out_shape=jax.ShapeDtypeStruct((n, A_Q_W), BF16),
        compiler_params=_cparams("parallel", "parallel", "parallel"),
        name="swa",
    )(sinks.reshape(1, A_HEADS), proj, proj, proj, proj, proj)


def _split3(x):
    hi = x.astype(BF16)
    r1 = x - hi.astype(F32)
    mid = r1.astype(BF16)
    lo = (r1 - mid.astype(F32)).astype(BF16)
    return hi, mid, lo


def _hgrn_kernel(lb_ref, q_ref, f_ref, i_ref, g_ref, o_ref, st_ref, *, ts):
    c = B_CHUNK
    nchunk = ts // c

    @pl.when(pl.program_id(2) == 0)
    def _():
        st_ref[...] = jnp.zeros_like(st_ref)

    lb = lb_ref[...]
    f = lb + (1.0 - lb) * _sigmoid(f_ref[...])
    k = 1.0 - f
    logf = jnp.log(f)

    ri = lax.broadcasted_iota(jnp.int32, (ts, ts), 0)
    ci = lax.broadcasted_iota(jnp.int32, (ts, ts), 1)
    same = (ri // c) == (ci // c)
    tri = (same & (ci <= ri)).astype(BF16)
    blk = same.astype(BF16)
    parts = _split3(logf)
    cum = sum(_dot(tri, p) for p in parts)
    tot = sum(_dot(blk, p) for p in parts)

    qd = _silu(q_ref[...]) * jnp.exp(cum)
    kinv = (k * jnp.exp(-cum)).astype(BF16)
    kend = (k * jnp.exp(tot - cum)).astype(BF16)
    dec = jnp.exp(tot)
    qd16 = qd.astype(BF16)
    v32 = i_ref[...]
    v16 = v32.astype(BF16)

    sub = LANES
    r2 = lax.broadcasted_iota(jnp.int32, (sub, sub), 0)
    c2 = lax.broadcasted_iota(jnp.int32, (sub, sub), 1)
    causal = ((r2 // c) == (c2 // c)) & (c2 <= r2)

    v_t = v32.T
    per_sub = sub // c
    outs = []
    kv_ts = []
    for sidx in range(ts // sub):
        sl = slice(sidx * sub, (sidx + 1) * sub)
        sc = jnp.where(causal, _dot_nt(qd16[sl], kinv[sl]), 0.0)
        outs.append(_dot(sc.astype(BF16), v16[sl]))
        v_ts = v_t[:, sl]
        lhs = jnp.concatenate(
            [jnp.where(c2 // c == jj, v_ts, 0.0) for jj in range(per_sub)], axis=0).astype(BF16)
        kv_ts.append(_dot(lhs, kend[sl]))
    o_intra = jnp.concatenate(outs, axis=0)

    st = st_ref[...]
    inter = []
    for ch in range(nchunk):
        sl = slice(ch * c, (ch + 1) * c)
        inter.append(_dot_nt(qd16[sl], st.astype(BF16)))
        jj = ch % per_sub
        kv_t = kv_ts[ch // per_sub][jj * B_DIM:(jj + 1) * B_DIM]
        st = st * dec[ch * c:ch * c + 1, :] + kv_t
    st_ref[...] = st
    o = o_intra + jnp.concatenate(inter, axis=0)

    o = o * lax.rsqrt(jnp.mean(o * o, axis=-1, keepdims=True) + EPS)
    o_ref[...] = (o * _silu(g_ref[...])).astype(o_ref.dtype)


def _hgrn(proj, lb, batch, seq, *, ts=512):
    n = proj.shape[0]
    nt = seq // ts
    base = (A_Q_W + 2 * A_KV_W) // LANES
    nh = B_W // LANES

    def col(kind):
        return lambda b, h, t: (b * nt + t, base + kind * nh + h)

    return pl.pallas_call(
        functools.partial(_hgrn_kernel, ts=ts),
        grid=(batch, B_HEADS, nt),
        in_specs=[
            pl.BlockSpec((1, LANES), lambda b, h, t: (0, h)),
            pl.BlockSpec((ts, LANES), col(0)),
            pl.BlockSpec((ts, LANES), col(1)),
            pl.BlockSpec((ts, LANES), col(2)),
            pl.BlockSpec((ts, LANES), col(3)),
        ],
        out_specs=pl.BlockSpec((ts, LANES), lambda b, h, t: (b * nt + t, h)),
        out_shape=jax.ShapeDtypeStruct((n, B_W), BF16),
        scratch_shapes=[pltpu.VMEM((B_DIM, B_DIM), F32)],
        compiler_params=_cparams("parallel", "parallel", "arbitrary"),
        name="hgrn2",
    )(lb.reshape(1, B_W), proj, proj, proj, proj)


def _mix_out_kernel(x_ref, a_ref, b_ref, wa_ref, wb_ref, o_ref):
    o_ref[...] = x_ref[...] + _dot(a_ref[...], wa_ref[...]) + _dot(b_ref[...], wb_ref[...])


def _mix_out(x, a, b, w, *, tm=1024):
    n, d = x.shape
    ka, kb = a.shape[1], b.shape[1]
    return pl.pallas_call(
        _mix_out_kernel,
        grid=(n // tm,),
        in_specs=[
            pl.BlockSpec((tm, d), lambda i: (i, 0)),
            pl.BlockSpec((tm, ka), lambda i: (i, 0)),
            pl.BlockSpec((tm, kb), lambda i: (i, 0)),
            pl.BlockSpec((ka, d), lambda i: (0, 0)),
            pl.BlockSpec((kb, d), lambda i: (1, 0)),
        ],
        out_specs=pl.BlockSpec((tm, d), lambda i: (i, 0)),
        out_shape=jax.ShapeDtypeStruct((n, d), F32),
        compiler_params=_cparams("parallel"),
        name="mix_out",
    )(x, a, b, w, w)


def _s5_rows(h_ref, nc):
    return jnp.concatenate(
        [h_ref[pl.ds(t, nc, stride=S5_T), :] for t in range(S5_T)], axis=1).astype(BF16)


def _s5_sum_kernel(h_ref, w_ref, o_ref, *, nc):
    o_ref[...] = _dot(_s5_rows(h_ref, nc), w_ref[0])


def _s5_sum(h, w, batch, seq):
    nc = seq // S5_T
    nblk = w.shape[0]
    return pl.pallas_call(
        functools.partial(_s5_sum_kernel, nc=nc),
        grid=(nblk, batch),
        in_specs=[pl.BlockSpec((seq, LANES), lambda j, b: (b, j)),
                  pl.BlockSpec((1,) + w.shape[1:], lambda j, b: (j, 0, 0))],
        out_specs=pl.BlockSpec((nc, w.shape[2]), lambda j, b: (b, j)),
        out_shape=jax.ShapeDtypeStruct((batch * nc, nblk * w.shape[2]), F32),
        compiler_params=_cparams("parallel", "parallel"),
        name="s5_chunk_sum",
    )(h, w)


def _s5_scan_kernel(s_ref, a1_ref, a2_ref, o_ref, x_ref, *, tc, nb):
    @pl.when(pl.program_id(0) == 0)
    def _():
        x_ref[...] = jnp.zeros_like(x_ref)

    a1 = a1_ref[...]
    a2 = a2_ref[...]

    def step(c, xs):
        new = []
        for b in range(nb):
            o_ref[b, c] = xs[b]
            new.append(a1 * xs[b] + a2 * pltpu.roll(xs[b], S5_STATE, axis=1) + s_ref[b, c])
        return tuple(new)

    xs = lax.fori_loop(0, tc, step, tuple(x_ref[b] for b in range(nb)))
    for b in range(nb):
        x_ref[b] = xs[b]


def _s5_scan(s, a1, a2, *, tc=32):
    nb, nc, rows, lanes = s.shape
    return pl.pallas_call(
        functools.partial(_s5_scan_kernel, tc=tc, nb=nb),
        grid=(nc // tc,),
        in_specs=[pl.BlockSpec((nb, tc, rows, lanes), lambda i: (0, i, 0, 0)),
                  pl.BlockSpec((rows, lanes), lambda i: (0, 0)),
                  pl.BlockSpec((rows, lanes), lambda i: (0, 0))],
        out_specs=pl.BlockSpec((nb, tc, rows, lanes), lambda i: (0, i, 0, 0)),
        out_shape=jax.ShapeDtypeStruct(s.shape, F32),
        scratch_shapes=[pltpu.VMEM((nb, rows, lanes), F32)],
        compiler_params=_cparams("arbitrary"),
        name="s5_chunk_scan",
    )(s, a1, a2)


def _s5_out_kernel(h_ref, x_ref, wt_ref, wc_ref, o_ref, *, nc, half_t):
    half = pl.program_id(2)
    y = _dot(_s5_rows(h_ref, nc), wt_ref[0]) + _dot(x_ref[...].astype(BF16), wc_ref[0])
    for tt in range(half_t):
        o_ref[pl.ds(half * half_t + tt, nc, stride=S5_T), :] = y[:, tt * LANES:(tt + 1) * LANES]


def _s5_out(h, xprev, w_toep, w_carry, batch, seq):
    nc = seq // S5_T
    nblk = w_toep.shape[0]
    k = w_toep.shape[1]
    kc = w_carry.shape[1]
    halves = 2
    half_t = S5_T // halves
    nh = half_t * LANES
    return pl.pallas_call(
        functools.partial(_s5_out_kernel, nc=nc, half_t=half_t),
        grid=(nblk, batch, halves),
        in_specs=[pl.BlockSpec((seq, LANES), lambda j, b, q: (b, j)),
                  pl.BlockSpec((nc, kc), lambda j, b, q: (b, j)),
                  pl.BlockSpec((1, k, nh), lambda j, b, q: (j, 0, q)),
                  pl.BlockSpec((1, kc, nh), lambda j, b, q: (j, 0, q))],
        out_specs=pl.BlockSpec((seq, LANES), lambda j, b, q: (b, j)),
        out_shape=jax.ShapeDtypeStruct(h.shape, F32),
        compiler_params=_cparams("parallel", "parallel", "arbitrary"),
        name="s5_chunk_out",
    )(h, xprev, w_toep, w_carry)


def _s5_glu_kernel(x_ref, h_ref, y_ref, d_ref, w_ref, o_ref):
    y = y_ref[...] + d_ref[...] * h_ref[...]
    inner = math.sqrt(2.0 / math.pi) * (y + 0.044715 * (y * y * y))
    ge = y * (0.5 * (1.0 + jnp.tanh(inner)))
    z = _dot(ge.astype(BF16), w_ref[...])
    d = x_ref.shape[1]
    o_ref[...] = x_ref[...] + z[:, :d] * _sigmoid(z[:, d:])


def _s5_glu(x, h, y, dskip, w, *, tm=512):
    n, d = x.shape
    tile = pl.BlockSpec((tm, d), lambda i: (i, 0))
    return pl.pallas_call(
        _s5_glu_kernel,
        grid=(n // tm,),
        in_specs=[tile, tile, tile,
                  pl.BlockSpec((1, d), lambda i: (0, 0)),
                  pl.BlockSpec((d, 2 * d), lambda i: (0, 0))],
        out_specs=tile,
        out_shape=jax.ShapeDtypeStruct((n, d), F32),
        compiler_params=_cparams("parallel"),
        name="s5_glu",
    )(x, h, y, dskip, w)


def _s5_tables(a_re, a_im, log_step, b_re, b_im, c_re, c_im):
    hp = lax.Precision.HIGHEST
    t = S5_T
    step = jnp.exp(log_step)[:, None]
    mag = jnp.exp(step * a_re)
    ab_r = mag * jnp.cos(step * a_im)
    ab_i = mag * jnp.sin(step * a_im)
    den = a_re * a_re + a_im * a_im
    coef_r = ((ab_r - 1.0) * a_re + ab_i * a_im) / den
    coef_i = (ab_i * a_re - (ab_r - 1.0) * a_im) / den
    bb_r = coef_r[..., None] * b_re - coef_i[..., None] * b_im
    bb_i = coef_r[..., None] * b_im + coef_i[..., None] * b_re
    j = jnp.arange(t + 1, dtype=F32)[:, None, None]
    pmag = jnp.exp(j * (step * a_re)[None])
    pw_r = pmag * jnp.cos(j * (step * a_im)[None])
    pw_i = pmag * jnp.sin(j * (step * a_im)[None])

    rev_r, rev_i = pw_r[t - 1::-1], pw_i[t - 1::-1]
    ws_r = rev_r[:, :, :, None] * bb_r[None] - rev_i[:, :, :, None] * bb_i[None]
    ws_i = rev_r[:, :, :, None] * bb_i[None] + rev_i[:, :, :, None] * bb_r[None]
    w_sum = jnp.concatenate([ws_r, ws_i], axis=2)
    w_sum = w_sum.transpose(1, 0, 3, 2).reshape(S5_GROUPS, S5_ROW, 2 * S5_STATE)

    lb_r = pw_r[:t, :, :, None] * bb_r[None] - pw_i[:t, :, :, None] * bb_i[None]
    lb_i = pw_r[:t, :, :, None] * bb_i[None] + pw_i[:t, :, :, None] * bb_r[None]
    kmat = (jnp.einsum('gdp,jgpc->jgdc', c_re, lb_r, precision=hp)
            - jnp.einsum('gdp,jgpc->jgdc', c_im, lb_i, precision=hp))
    s_idx = jnp.arange(t)[:, None]
    t_idx = jnp.arange(t)[None, :]
    lag = t_idx - s_idx
    toep = kmat[jnp.clip(lag, 0, t - 1)]
    toep = jnp.where((lag >= 0)[:, :, None, None, None], toep, 0.0)
    w_toep = toep.transpose(2, 0, 4, 1, 3).reshape(S5_GROUPS, S5_ROW, S5_ROW)

    q_r, q_i = pw_r[1:], pw_i[1:]
    wc_r = c_re[None] * q_r[:, :, None, :] - c_im[None] * q_i[:, :, None, :]
    wc_i = -(c_re[None] * q_i[:, :, None, :] + c_im[None] * q_r[:, :, None, :])
    w_carry = jnp.concatenate([wc_r, wc_i], axis=3)
    w_carry = w_carry.transpose(1, 3, 0, 2).reshape(S5_GROUPS, 2 * S5_STATE, S5_ROW)

    a1 = jnp.concatenate([pw_r[t], pw_r[t]], axis=1)
    a2 = jnp.concatenate([-pw_i[t], pw_i[t]], axis=1)

    gl = LANES // S5_GROUP
    nblk = S5_GROUPS // gl
    eye = jnp.eye(gl, dtype=BF16)
    ws = w_sum.astype(BF16).reshape(nblk, gl, t, S5_GROUP, 2 * S5_STATE).transpose(0, 2, 1, 3, 4)
    ws = ws[:, :, :, :, None, :] * eye[None, None, :, None, :, None]
    ws = ws.reshape(nblk, t * LANES, gl * 2 * S5_STATE)
    wt = w_toep.astype(BF16).reshape(nblk, gl, t, S5_GROUP, t, S5_GROUP).transpose(0, 2, 1, 3, 4, 5)
    wt = wt[:, :, :, :, :, None, :] * eye[None, None, :, None, None, :, None]
    wt = wt.reshape(nblk, t * LANES, t * LANES)
    wc = w_carry.astype(BF16).reshape(nblk, gl, 2 * S5_STATE, t, S5_GROUP)
    wc = wc[:, :, :, :, None, :] * eye[None, :, None, None, :, None]
    wc = wc.reshape(nblk, gl * 2 * S5_STATE, t * LANES)
    return ws, wt, wc, a1, a2


def _s5_mixer(x, g, tables, dskip, w_glu, batch, seq):
    w_sum, w_toep, w_carry, a1, a2 = tables
    nc = seq // S5_T
    h = _norm(x, g)
    s = _s5_sum(h, w_sum, batch, seq)
    s4 = s.reshape(batch, nc, S5_GROUPS, 2 * S5_STATE)
    xprev = _s5_scan(s4, a1, a2).reshape(s.shape)
    y = _s5_out(h, xprev, w_toep, w_carry, batch, seq)
    return _s5_glu(x, h, y, dskip, w_glu)


def kernel(x, norm_g, ffn_w_in, ffn_w_out, mix_w_in, attn_sinks, hgrn_lb, mix_w_out, s5_a_re, s5_a_im, s5_log_step, s5_b_re, s5_b_im, s5_c_re, s5_c_im, s5_d, s5_w_glu, final_g):
    batch, seq, d = x.shape
    n = batch * seq
    x = x.reshape(n, d)

    w_in16 = ffn_w_in.astype(BF16)
    w_out16 = ffn_w_out.astype(BF16)
    mix_in16 = mix_w_in.astype(BF16)
    mix_out16 = mix_w_out.astype(BF16)
    glu16 = s5_w_glu.astype(BF16)
    fg = final_g.reshape(1, d)

    lb_p = jax.nn.softmax(hgrn_lb.astype(F32), axis=0)
    lb_all = jnp.cumsum(lb_p, axis=0) - lb_p[0]

    for layer in range(DEPTH):
        g3 = norm_g[layer].reshape(3, 1, d)
        x = _ffn(x, g3[0], w_in16[layer, 0], w_out16[layer, 0], fg, final=False)
        if layer % 2 == 0:
            e = layer // 2
            proj = _norm_proj(x, g3[1], mix_in16[e])
            attn = _attention(proj, attn_sinks[e], batch, seq)
            rec = _hgrn(proj, lb_all[e], batch, seq)
            x = _mix_out(x, attn, rec, mix_out16[e])
        else:
            o = layer // 2
            tables = _s5_tables(s5_a_re[o], s5_a_im[o], s5_log_step[o], s5_b_re[o], s5_b_im[o],
                                s5_c_re[o], s5_c_im[o])
            x = _s5_mixer(x, g3[1], tables, s5_d[o].reshape(1, d), glu16[o], batch, seq)
        x = _ffn(x, g3[2], w_in16[layer, 1], w_out16[layer, 1], fg, final=(layer == DEPTH - 1))
    return x.reshape(batch, seq, d)
```

```python
import functools
import math

import jax
import jax.numpy as jnp
import numpy as np
from jax import lax
from jax.experimental import pallas as pl
from jax.experimental.pallas import tpu as pltpu

F32 = jnp.float32
BF16 = jnp.bfloat16

EPS = 1e-6
DEPTH = 4
D_MODEL = 1024
D_FF = 2816
A_HEADS = 8
A_KV_HEADS = 2
A_GROUP = A_HEADS // A_KV_HEADS
A_HEAD_DIM = 64
WINDOW = 128
A_Q_W = A_HEADS * A_HEAD_DIM
A_KV_W = A_KV_HEADS * A_HEAD_DIM
B_HEADS = 4
B_DIM = 128
B_CHUNK = 32
B_W = B_HEADS * B_DIM
MIX_IN = A_Q_W + 2 * A_KV_W + 4 * B_W
S5_GROUP = 16
S5_GROUPS = D_MODEL // S5_GROUP
S5_STATE = 64
S5_T = 16
S5_ROW = S5_T * S5_GROUP

LANES = 128
VMEM_LIMIT = 48 * 1024 * 1024


def _cparams(*sem):
    return pltpu.CompilerParams(dimension_semantics=sem, vmem_limit_bytes=VMEM_LIMIT)


def _sigmoid(x):
    return 1.0 / (1.0 + jnp.exp(-x))


def _silu(x):
    return x * _sigmoid(x)


def _rms(x, g):
    return x * lax.rsqrt(jnp.mean(x * x, axis=-1, keepdims=True) + EPS) * g


def _dot(a, b):
    return jnp.dot(a, b, preferred_element_type=F32)


def _dot_nt(a, b):
    return lax.dot_general(a, b, (((1,), (1,)), ((), ())), preferred_element_type=F32)


def _ffn_kernel(x_ref, g_ref, wi_ref, wo_ref, fg_ref, o_ref, *, tf, final):
    x = x_ref[...]
    h = _rms(x, g_ref[...]).astype(BF16)
    acc = None
    for c in range(D_FF // tf):
        gate = _dot(h, wi_ref[:, c * tf:(c + 1) * tf])
        up = _dot(h, wi_ref[:, D_FF + c * tf:D_FF + (c + 1) * tf])
        act = (_silu(gate) * up).astype(BF16)
        part = _dot(act, wo_ref[c * tf:(c + 1) * tf, :])
        acc = part if acc is None else acc + part
    y = x + 0.5 * acc
    if final:
        y = _rms(y, fg_ref[...])
    o_ref[...] = y


def _resident(shape):
    return pl.BlockSpec(shape, lambda *_: (0,) * len(shape), pipeline_mode=pl.Buffered(1))


def _ffn(x, g, w_in, w_out, final_g, *, final, tm=512, tf=256):
    n, d = x.shape
    return pl.pallas_call(
        functools.partial(_ffn_kernel, tf=tf, final=final),
        grid=(n // tm,),
        in_specs=[
            pl.BlockSpec((tm, d), lambda i: (i, 0)),
            _resident((1, d)),
            _resident((d, 2 * D_FF)),
            _resident((D_FF, d)),
            _resident((1, d)),
        ],
        out_specs=pl.BlockSpec((tm, d), lambda i: (i, 0)),
        out_shape=jax.ShapeDtypeStruct((n, d), F32),
        compiler_params=_cparams("parallel"),
        name="ffn",
    )(x, g, w_in, w_out, final_g)


def _norm_proj_kernel(x_ref, g_ref, w_ref, o_ref):
    h = _rms(x_ref[...], g_ref[...]).astype(BF16)
    o_ref[...] = _dot(h, w_ref[...])


def _norm_proj(x, g, w, *, tm=512):
    n, d = x.shape
    m = w.shape[1]
    return pl.pallas_call(
        _norm_proj_kernel,
        grid=(n // tm,),
        in_specs=[
            pl.BlockSpec((tm, d), lambda i: (i, 0)),
            pl.BlockSpec((1, d), lambda i: (0, 0)),
            pl.BlockSpec((d, m), lambda i: (0, 0)),
        ],
        out_specs=pl.BlockSpec((tm, m), lambda i: (i, 0)),
        out_shape=jax.ShapeDtypeStruct((n, m), F32),
        compiler_params=_cparams("parallel"),
        name="norm_proj",
    )(x, g, w)


def _norm_kernel(x_ref, g_ref, o_ref):
    o_ref[...] = _rms(x_ref[...], g_ref[...])


def _norm(x, g, *, tm=1024):
    n, d = x.shape
    return pl.pallas_call(
        _norm_kernel,
        grid=(n // tm,),
        in_specs=[pl.BlockSpec((tm, d), lambda i: (i, 0)), pl.BlockSpec((1, d), lambda i: (0, 0))],
        out_specs=pl.BlockSpec((tm, d), lambda i: (i, 0)),
        out_shape=jax.ShapeDtypeStruct((n, d), F32),
        compiler_params=_cparams("parallel"),
        name="norm",
    )(x, g)


def _attn_bias():
    w = WINDOW
    slopes = 2.0 ** (-8.0 * np.arange(1, A_HEADS + 1) / A_HEADS)
    dist = (np.arange(w) + w)[:, None] - np.arange(2 * w)[None, :]
    band = (dist >= 0) & (dist < w)
    has_prev = np.arange(2 * w)[None, :] >= w
    val = -slopes.reshape(A_KV_HEADS, A_GROUP, 1, 1) * dist.astype(np.float64)[None, None]
    out = np.stack([np.where(band & has_prev, val, -np.inf), np.where(band, val, -np.inf)])
    return jnp.asarray(out.reshape(2, A_KV_HEADS, A_GROUP * w, 2 * w), dtype=F32)


def _attn_kernel(sink_ref, bias_ref, q_ref, kc_ref, kp_ref, vc_ref, vp_ref, o_ref, *, nq):
    hk = pl.program_id(1)
    tile = pl.program_id(2)
    w = WINDOW
    dh = A_HEAD_DIM
    qw = A_GROUP * dh

    def widen(prev_ref, cur_ref):
        t = jnp.concatenate([prev_ref[...], cur_ref[...]], axis=0)
        r = pltpu.roll(t, dh, axis=1)
        lane = lax.broadcasted_iota(jnp.int32, t.shape, 1)
        t2 = jnp.where(lane // dh == hk, t, r)
        return jnp.concatenate([t2, t2], axis=1).astype(BF16)

    k4 = widen(kp_ref, kc_ref)
    v4 = widen(vp_ref, vc_ref)

    lane_head = lax.broadcasted_iota(jnp.int32, (w, qw), 1) // dh
    grp = lax.broadcasted_iota(jnp.int32, (A_GROUP * w, 1), 0) // w
    sink = jnp.zeros((A_GROUP * w, 1), F32)
    for g in range(A_GROUP):
        sink = jnp.where(grp == g, sink_ref[0, hk * A_GROUP + g], sink)

    bias_rest = bias_ref[1, 0]
    for blk in range(nq):
        q = q_ref[blk * w:(blk + 1) * w, :] * (dh ** -0.5)
        q4 = jnp.concatenate(
            [jnp.where(lane_head == g, q, 0.0) for g in range(A_GROUP)], axis=0).astype(BF16)
        keys = slice(blk * w, (blk + 2) * w)
        bias = jnp.where(tile == 0, bias_ref[0, 0], bias_rest) if blk == 0 else bias_rest
        s = _dot_nt(q4, k4[keys]) + bias
        m = jnp.maximum(jnp.max(s, axis=-1, keepdims=True), sink)
        p = jnp.exp(s - m)
        denom = jnp.sum(p, axis=-1, keepdims=True) + jnp.exp(sink - m)
        o4 = _dot(p.astype(BF16), v4[keys]) / denom
        o = jnp.zeros((w, qw), F32)
        for g in range(A_GROUP):
            o = o + jnp.where(lane_head == g, o4[g * w:(g + 1) * w, :], 0.0)
        o_ref[blk * w:(blk + 1) * w, :] = o.astype(o_ref.dtype)


def _attention(proj, sinks, batch, seq, *, nq=4):
    n = proj.shape[0]
    rows = nq * WINDOW
    nt = seq // rows
    qw = A_GROUP * A_HEAD_DIM
    kcol = A_Q_W // LANES
    vcol = (A_Q_W + A_KV_W) // LANES
    cur = lambda b, h, i: b * nt + i
    prev = lambda b, h, i: b * nt * nq + jnp.maximum(i * nq - 1, 0)
    bias = _attn_bias()
    return pl.pallas_call(
        functools.partial(_attn_kernel, nq=nq),
        grid=(batch, A_KV_HEADS, nt),
        in_specs=[
            pl.BlockSpec(memory_space=pltpu.SMEM),
            pl.BlockSpec((2, 1) + bias.shape[2:], lambda b, h, i: (0, h, 0, 0)),
            pl.BlockSpec((rows, qw), lambda b, h, i: (cur(b, h, i), h)),
            pl.BlockSpec((rows, LANES), lambda b, h, i: (cur(b, h, i), kcol)),
            pl.BlockSpec((WINDOW, LANES), lambda b, h, i: (prev(b, h, i), kcol)),
            pl.BlockSpec((rows, LANES), lambda b, h, i: (cur(b, h, i), vcol)),
            pl.BlockSpec((WINDOW, LANES), lambda b, h, i: (prev(b, h, i), vcol)),
        ],
        out_specs=pl.BlockSpec((rows, qw), lambda b, h, i: (cur(b, h, i), h)),
        out_shape=jax.ShapeDtypeStruct((n, A_Q_W), BF16),
        compiler_params=_cparams("parallel", "parallel", "parallel"),
        name="swa",
    )(sinks.reshape(1, A_HEADS), bias, proj, proj, proj, proj, proj)


def _split3(x):
    hi = x.astype(BF16)
    r1 = x - hi.astype(F32)
    mid = r1.astype(BF16)
    lo = (r1 - mid.astype(F32)).astype(BF16)
    return hi, mid, lo


def _hgrn_kernel(lb_ref, q_ref, f_ref, i_ref, g_ref, o_ref, st_ref, *, ts):
    c = B_CHUNK
    nchunk = ts // c

    @pl.when(pl.program_id(2) == 0)
    def _():
        st_ref[...] = jnp.zeros_like(st_ref)

    lb = lb_ref[...]
    f = lb + (1.0 - lb) * _sigmoid(f_ref[...])
    k = 1.0 - f
    logf = jnp.log(f)

    sub = LANES
    r2 = lax.broadcasted_iota(jnp.int32, (sub, sub), 0)
    c2 = lax.broadcasted_iota(jnp.int32, (sub, sub), 1)
    causal = ((r2 // c) == (c2 // c)) & (c2 <= r2)

    tri = causal.astype(BF16)
    parts = _split3(logf)
    cum = jnp.concatenate(
        [sum(_dot(tri, p[s0:s0 + sub]) for p in parts) for s0 in range(0, ts, sub)], axis=0)
    tot_rows = [cum[(ch + 1) * c - 1:(ch + 1) * c, :] for ch in range(nchunk)]
    tot = jnp.concatenate([jnp.broadcast_to(t, (c, B_DIM)) for t in tot_rows], axis=0)
    dec_rows = [jnp.exp(t) for t in tot_rows]

    qd = _silu(q_ref[...]) * jnp.exp(cum)
    kinv = (k * jnp.exp(-cum)).astype(BF16)
    kend = (k * jnp.exp(tot - cum)).astype(BF16)
    qd16 = qd.astype(BF16)
    v32 = i_ref[...]
    v16 = v32.astype(BF16)

    v_t = v32.T
    per_sub = sub // c
    outs = []
    kv_ts = []
    for sidx in range(ts // sub):
        sl = slice(sidx * sub, (sidx + 1) * sub)
        sc = jnp.where(causal, _dot_nt(qd16[sl], kinv[sl]), 0.0)
        outs.append(_dot(sc.astype(BF16), v16[sl]))
        v_ts = v_t[:, sl]
        lhs = jnp.concatenate(
            [jnp.where(c2 // c == jj, v_ts, 0.0) for jj in range(per_sub)], axis=0).astype(BF16)
        kv_ts.append(_dot(lhs, kend[sl]))
    o_intra = jnp.concatenate(outs, axis=0)

    st = st_ref[...]
    inter = []
    for ch in range(nchunk):
        sl = slice(ch * c, (ch + 1) * c)
        inter.append(_dot_nt(qd16[sl], st.astype(BF16)))
        jj = ch % per_sub
        kv_t = kv_ts[ch // per_sub][jj * B_DIM:(jj + 1) * B_DIM]
        st = st * dec_rows[ch] + kv_t
    st_ref[...] = st
    o = o_intra + jnp.concatenate(inter, axis=0)

    o = o * lax.rsqrt(jnp.mean(o * o, axis=-1, keepdims=True) + EPS)
    o_ref[...] = (o * _silu(g_ref[...])).astype(o_ref.dtype)


def _hgrn(proj, lb, batch, seq, *, ts=512):
    n = proj.shape[0]
    nt = seq // ts
    base = (A_Q_W + 2 * A_KV_W) // LANES
    nh = B_W // LANES

    def col(kind):
        return lambda b, h, t: (b * nt + t, base + kind * nh + h)

    return pl.pallas_call(
        functools.partial(_hgrn_kernel, ts=ts),
        grid=(batch, B_HEADS, nt),
        in_specs=[
            pl.BlockSpec((1, LANES), lambda b, h, t: (0, h)),
            pl.BlockSpec((ts, LANES), col(0)),
            pl.BlockSpec((ts, LANES), col(1)),
            pl.BlockSpec((ts, LANES), col(2)),
            pl.BlockSpec((ts, LANES), col(3)),
        ],
        out_specs=pl.BlockSpec((ts, LANES), lambda b, h, t: (b * nt + t, h)),
        out_shape=jax.ShapeDtypeStruct((n, B_W), BF16),
        scratch_shapes=[pltpu.VMEM((B_DIM, B_DIM), F32)],
        compiler_params=_cparams("parallel", "parallel", "arbitrary"),
        name="hgrn2",
    )(lb.reshape(1, B_W), proj, proj, proj, proj)


def _mix_out_kernel(x_ref, a_ref, b_ref, wa_ref, wb_ref, o_ref):
    o_ref[...] = x_ref[...] + _dot(a_ref[...], wa_ref[...]) + _dot(b_ref[...], wb_ref[...])


def _mix_out(x, a, b, w, *, tm=1024):
    n, d = x.shape
    ka, kb = a.shape[1], b.shape[1]
    return pl.pallas_call(
        _mix_out_kernel,
        grid=(n // tm,),
        in_specs=[
            pl.BlockSpec((tm, d), lambda i: (i, 0)),
            pl.BlockSpec((tm, ka), lambda i: (i, 0)),
            pl.BlockSpec((tm, kb), lambda i: (i, 0)),
            pl.BlockSpec((ka, d), lambda i: (0, 0)),
            pl.BlockSpec((kb, d), lambda i: (1, 0)),
        ],
        out_specs=pl.BlockSpec((tm, d), lambda i: (i, 0)),
        out_shape=jax.ShapeDtypeStruct((n, d), F32),
        compiler_params=_cparams("parallel"),
        name="mix_out",
    )(x, a, b, w, w)


def _s5_rows(h_ref, nc):
    return jnp.concatenate(
        [h_ref[pl.ds(t, nc, stride=S5_T), :] for t in range(S5_T)], axis=1).astype(BF16)


def _s5_sum_kernel(h_ref, w_ref, o_ref, *, nc):
    o_ref[...] = _dot(_s5_rows(h_ref, nc), w_ref[0])


def _s5_sum(h, w, batch, seq):
    nc = seq // S5_T
    nblk = w.shape[0]
    return pl.pallas_call(
        functools.partial(_s5_sum_kernel, nc=nc),
        grid=(nblk, batch),
        in_specs=[pl.BlockSpec((seq, LANES), lambda j, b: (b, j)),
                  pl.BlockSpec((1,) + w.shape[1:], lambda j, b: (j, 0, 0))],
        out_specs=pl.BlockSpec((nc, w.shape[2]), lambda j, b: (b, j)),
        out_shape=jax.ShapeDtypeStruct((batch * nc, nblk * w.shape[2]), F32),
        compiler_params=_cparams("parallel", "parallel"),
        name="s5_chunk_sum",
    )(h, w)


def _s5_scan_kernel(s_ref, a1_ref, a2_ref, o_ref, x_ref, *, tc, nb):
    @pl.when(pl.program_id(0) == 0)
    def _():
        x_ref[...] = jnp.zeros_like(x_ref)

    a1 = a1_ref[...]
    a2 = a2_ref[...]

    def step(c, xs):
        new = []
        for b in range(nb):
            o_ref[b, c] = xs[b]
            new.append(a1 * xs[b] + a2 * pltpu.roll(xs[b], S5_STATE, axis=1) + s_ref[b, c])
        return tuple(new)

    xs = lax.fori_loop(0, tc, step, tuple(x_ref[b] for b in range(nb)))
    for b in range(nb):
        x_ref[b] = xs[b]


def _s5_scan(s, a1, a2, *, tc=32):
    nb, nc, rows, lanes = s.shape
    return pl.pallas_call(
        functools.partial(_s5_scan_kernel, tc=tc, nb=nb),
        grid=(nc // tc,),
        in_specs=[pl.BlockSpec((nb, tc, rows, lanes), lambda i: (0, i, 0, 0)),
                  pl.BlockSpec((rows, lanes), lambda i: (0, 0)),
                  pl.BlockSpec((rows, lanes), lambda i: (0, 0))],
        out_specs=pl.BlockSpec((nb, tc, rows, lanes), lambda i: (0, i, 0, 0)),
        out_shape=jax.ShapeDtypeStruct(s.shape, F32),
        scratch_shapes=[pltpu.VMEM((nb, rows, lanes), F32)],
        compiler_params=_cparams("arbitrary"),
        name="s5_chunk_scan",
    )(s, a1, a2)


def _s5_out_kernel(h_ref, x_ref, lag_ref, wc_ref, o_ref, wt_ref, *, nc):
    t = S5_T
    half_t = t // 2
    half = half_t * LANES

    @pl.when(pl.program_id(1) == 0)
    def _():
        zero = jnp.zeros((LANES, LANES), BF16)
        for s in range(t):
            for tp in range(half_t if s >= half_t else 0, t):
                blk = lag_ref[0, tp - s] if tp >= s else zero
                wt_ref[s * LANES:(s + 1) * LANES, tp * LANES:(tp + 1) * LANES] = blk

    a = _s5_rows(h_ref, nc)
    xp = x_ref[...].astype(BF16)
    y0 = _dot(a[:, :half], wt_ref[:half, :half]) + _dot(xp, wc_ref[0, :, :half])
    y1 = _dot(a, wt_ref[:, half:]) + _dot(xp, wc_ref[0, :, half:])
    for tt in range(half_t):
        o_ref[pl.ds(tt, nc, stride=t), :] = y0[:, tt * LANES:(tt + 1) * LANES]
        o_ref[pl.ds(half_t + tt, nc, stride=t), :] = y1[:, tt * LANES:(tt + 1) * LANES]


def _s5_out(h, xprev, w_lag, w_carry, batch, seq):
    nc = seq // S5_T
    nblk = w_lag.shape[0]
    kc = w_carry.shape[1]
    k = S5_T * LANES
    return pl.pallas_call(
        functools.partial(_s5_out_kernel, nc=nc),
        grid=(nblk, batch),
        in_specs=[pl.BlockSpec((seq, LANES), lambda j, b: (b, j)),
                  pl.BlockSpec((nc, kc), lambda j, b: (b, j)),
                  pl.BlockSpec((1,) + w_lag.shape[1:], lambda j, b: (j, 0, 0, 0)),
                  pl.BlockSpec((1, kc, k), lambda j, b: (j, 0, 0))],
        out_specs=pl.BlockSpec((seq, LANES), lambda j, b: (b, j)),
        out_shape=jax.ShapeDtypeStruct(h.shape, F32),
        scratch_shapes=[pltpu.VMEM((k, k), BF16)],
        compiler_params=_cparams("parallel", "arbitrary"),
        name="s5_chunk_out",
    )(h, xprev, w_lag, w_carry)


def _s5_glu_kernel(x_ref, h_ref, y_ref, d_ref, w_ref, o_ref):
    y = y_ref[...] + d_ref[...] * h_ref[...]
    inner = math.sqrt(2.0 / math.pi) * (y + 0.044715 * (y * y * y))
    ge = y * (0.5 * (1.0 + jnp.tanh(inner)))
    z = _dot(ge.astype(BF16), w_ref[...])
    d = x_ref.shape[1]
    o_ref[...] = x_ref[...] + z[:, :d] * _sigmoid(z[:, d:])


def _s5_glu(x, h, y, dskip, w, *, tm=512):
    n, d = x.shape
    tile = pl.BlockSpec((tm, d), lambda i: (i, 0))
    return pl.pallas_call(
        _s5_glu_kernel,
        grid=(n // tm,),
        in_specs=[tile, tile, tile,
                  pl.BlockSpec((1, d), lambda i: (0, 0)),
                  pl.BlockSpec((d, 2 * d), lambda i: (0, 0))],
        out_specs=tile,
        out_shape=jax.ShapeDtypeStruct((n, d), F32),
        compiler_params=_cparams("parallel"),
        name="s5_glu",
    )(x, h, y, dskip, w)


def _s5_tables(a_re, a_im, log_step, b_re, b_im, c_re, c_im):
    hp = lax.Precision.HIGHEST
    t = S5_T
    step = jnp.exp(log_step)[:, None]
    mag = jnp.exp(step * a_re)
    ab_r = mag * jnp.cos(step * a_im)
    ab_i = mag * jnp.sin(step * a_im)
    den = a_re * a_re + a_im * a_im
    coef_r = ((ab_r - 1.0) * a_re + ab_i * a_im) / den
    coef_i = (ab_i * a_re - (ab_r - 1.0) * a_im) / den
    bb_r = coef_r[..., None] * b_re - coef_i[..., None] * b_im
    bb_i = coef_r[..., None] * b_im + coef_i[..., None] * b_re
    j = jnp.arange(t + 1, dtype=F32)[:, None, None]
    pmag = jnp.exp(j * (step * a_re)[None])
    pw_r = pmag * jnp.cos(j * (step * a_im)[None])
    pw_i = pmag * jnp.sin(j * (step * a_im)[None])

    rev_r, rev_i = pw_r[t - 1::-1], pw_i[t - 1::-1]
    ws_r = rev_r[:, :, :, None] * bb_r[None] - rev_i[:, :, :, None] * bb_i[None]
    ws_i = rev_r[:, :, :, None] * bb_i[None] + rev_i[:, :, :, None] * bb_r[None]
    w_sum = jnp.concatenate([ws_r, ws_i], axis=2)
    w_sum = w_sum.transpose(1, 0, 3, 2).reshape(S5_GROUPS, S5_ROW, 2 * S5_STATE)

    lb_r = pw_r[:t, :, :, None] * bb_r[None] - pw_i[:t, :, :, None] * bb_i[None]
    lb_i = pw_r[:t, :, :, None] * bb_i[None] + pw_i[:t, :, :, None] * bb_r[None]
    kmat = (jnp.einsum('gdp,jgpc->jgdc', c_re, lb_r, precision=hp)
            - jnp.einsum('gdp,jgpc->jgdc', c_im, lb_i, precision=hp))

    q_r, q_i = pw_r[1:], pw_i[1:]
    wc_r = c_re[None] * q_r[:, :, None, :] - c_im[None] * q_i[:, :, None, :]
    wc_i = -(c_re[None] * q_i[:, :, None, :] + c_im[None] * q_r[:, :, None, :])
    w_carry = jnp.concatenate([wc_r, wc_i], axis=3)
    w_carry = w_carry.transpose(1, 3, 0, 2).reshape(S5_GROUPS, 2 * S5_STATE, S5_ROW)

    a1 = jnp.concatenate([pw_r[t], pw_r[t]], axis=1)
    a2 = jnp.concatenate([-pw_i[t], pw_i[t]], axis=1)

    gl = LANES // S5_GROUP
    nblk = S5_GROUPS // gl
    eye = jnp.eye(gl, dtype=BF16)
    ws = w_sum.astype(BF16).reshape(nblk, gl, t, S5_GROUP, 2 * S5_STATE).transpose(0, 2, 1, 3, 4)
    ws = ws[:, :, :, :, None, :] * eye[None, None, :, None, :, None]
    ws = ws.reshape(nblk, t * LANES, gl * 2 * S5_STATE)

    lane = jnp.arange(LANES)
    spread = (lane[None, :] % S5_GROUP == jnp.arange(S5_GROUP)[:, None]).astype(BF16)
    lane_group = lane // S5_GROUP

    def block_diag(tbl, rows_per_group):
        wide = jnp.einsum('...rc,cl->...rl', tbl.astype(BF16), spread, preferred_element_type=F32)
        row_group = (jnp.arange(tbl.shape[-2]) // rows_per_group) % gl
        return jnp.where(row_group[:, None] == lane_group[None, :], wide, 0.0).astype(BF16)

    lag = block_diag(kmat.transpose(0, 1, 3, 2).reshape(t, S5_GROUPS * S5_GROUP, S5_GROUP), S5_GROUP)
    lag = lag.reshape(t, nblk, LANES, LANES).transpose(1, 0, 2, 3)
    wc = w_carry.reshape(S5_GROUPS * 2 * S5_STATE, t, S5_GROUP).transpose(1, 0, 2)
    wc = block_diag(wc, 2 * S5_STATE)
    wc = wc.reshape(t, nblk, gl * 2 * S5_STATE, LANES).transpose(1, 2, 0, 3)
    wc = wc.reshape(nblk, gl * 2 * S5_STATE, t * LANES)
    return ws, lag, wc, a1, a2


def _s5_mixer(x, g, tables, dskip, w_glu, batch, seq):
    w_sum, w_lag, w_carry, a1, a2 = tables
    nc = seq // S5_T
    h = _norm(x, g)
    s = _s5_sum(h, w_sum, batch, seq)
    s4 = s.reshape(batch, nc, S5_GROUPS, 2 * S5_STATE)
    xprev = _s5_scan(s4, a1, a2).reshape(s.shape)
    y = _s5_out(h, xprev, w_lag, w_carry, batch, seq)
    return _s5_glu(x, h, y, dskip, w_glu)


def kernel(x, norm_g, ffn_w_in, ffn_w_out, mix_w_in, attn_sinks, hgrn_lb, mix_w_out, s5_a_re, s5_a_im, s5_log_step, s5_b_re, s5_b_im, s5_c_re, s5_c_im, s5_d, s5_w_glu, final_g):
    batch, seq, d = x.shape
    n = batch * seq
    x = x.reshape(n, d)

    w_in16 = ffn_w_in.astype(BF16)
    w_out16 = ffn_w_out.astype(BF16)
    mix_in16 = mix_w_in.astype(BF16)
    mix_out16 = mix_w_out.astype(BF16)
    glu16 = s5_w_glu.astype(BF16)
    fg = final_g.reshape(1, d)

    lb_p = jax.nn.softmax(hgrn_lb.astype(F32), axis=0)
    lb_all = jnp.cumsum(lb_p, axis=0) - lb_p[0]

    for layer in range(DEPTH):
        g3 = norm_g[layer].reshape(3, 1, d)
        x = _ffn(x, g3[0], w_in16[layer, 0], w_out16[layer, 0], fg, final=False)
        if layer % 2 == 0:
            e = layer // 2
            proj = _norm_proj(x, g3[1], mix_in16[e])
            attn = _attention(proj, attn_sinks[e], batch, seq)
            rec = _hgrn(proj, lb_all[e], batch, seq)
            x = _mix_out(x, attn, rec, mix_out16[e])
        else:
            o = layer // 2
            tables = _s5_tables(s5_a_re[o], s5_a_im[o], s5_log_step[o], s5_b_re[o], s5_b_im[o],
                                s5_c_re[o], s5_c_im[o])
            x = _s5_mixer(x, g3[1], tables, s5_d[o].reshape(1, d), glu16[o], batch, seq)
        x = _ffn(x, g3[2], w_in16[layer, 1], w_out16[layer, 1], fg, final=(layer == DEPTH - 1))
    return x.reshape(batch, seq, d)
```

```python
import functools
import math

import jax
import jax.numpy as jnp
import numpy as np
from jax import lax
from jax.experimental import pallas as pl
from jax.experimental.pallas import tpu as pltpu

F32 = jnp.float32
BF16 = jnp.bfloat16

EPS = 1e-6
DEPTH = 4
D_MODEL = 1024
D_FF = 2816
A_HEADS = 8
A_KV_HEADS = 2
A_GROUP = A_HEADS // A_KV_HEADS
A_HEAD_DIM = 64
WINDOW = 128
A_Q_W = A_HEADS * A_HEAD_DIM
A_KV_W = A_KV_HEADS * A_HEAD_DIM
B_HEADS = 4
B_DIM = 128
B_CHUNK = 32
B_W = B_HEADS * B_DIM
MIX_IN = A_Q_W + 2 * A_KV_W + 4 * B_W
S5_GROUP = 16
S5_GROUPS = D_MODEL // S5_GROUP
S5_STATE = 64
S5_T = 16
S5_ROW = S5_T * S5_GROUP

LANES = 128
VMEM_LIMIT = 48 * 1024 * 1024


def _cparams(*sem):
    return pltpu.CompilerParams(dimension_semantics=sem, vmem_limit_bytes=VMEM_LIMIT)


def _sigmoid(x):
    return 1.0 / (1.0 + jnp.exp(-x))


def _silu(x):
    return x * _sigmoid(x)


def _rms(x, g):
    return x * lax.rsqrt(jnp.mean(x * x, axis=-1, keepdims=True) + EPS) * g


def _dot(a, b):
    return jnp.dot(a, b, preferred_element_type=F32)


def _dot_nt(a, b):
    return lax.dot_general(a, b, (((1,), (1,)), ((), ())), preferred_element_type=F32)


def _ffn_kernel(x_ref, g_ref, wi_ref, wo_ref, g2_ref, *o_refs, tf, post):
    x = x_ref[...]
    h = _rms(x, g_ref[...]).astype(BF16)
    acc = None
    for c in range(D_FF // tf):
        gate = _dot(h, wi_ref[:, c * tf:(c + 1) * tf])
        up = _dot(h, wi_ref[:, D_FF + c * tf:D_FF + (c + 1) * tf])
        act = (_silu(gate) * up).astype(BF16)
        part = _dot(act, wo_ref[c * tf:(c + 1) * tf, :])
        acc = part if acc is None else acc + part
    y = x + 0.5 * acc
    if post == "norm_only":
        o_refs[0][...] = _rms(y, g2_ref[...])
    else:
        o_refs[0][...] = y
        if post == "also_norm":
            o_refs[1][...] = _rms(y, g2_ref[...])


def _resident(shape, *index):
    lead = len(index)
    block = (None,) * lead + tuple(shape)
    full = tuple(index) + (0,) * len(shape)
    return pl.BlockSpec(block, lambda *_: full, pipeline_mode=pl.Buffered(1))


def _ffn(x, g, w_in, w_out, layer, which, g2, *, post="none", tm=512, tf=256):
    n, d = x.shape
    tile = pl.BlockSpec((tm, d), lambda i: (i, 0))
    shape = jax.ShapeDtypeStruct((n, d), F32)
    two = post == "also_norm"
    return pl.pallas_call(
        functools.partial(_ffn_kernel, tf=tf, post=post),
        grid=(n // tm,),
        in_specs=[
            tile,
            _resident((1, d)),
            _resident((d, 2 * D_FF), layer, which),
            _resident((D_FF, d), layer, which),
            _resident((1, d)),
        ],
        out_specs=[tile, tile] if two else tile,
        out_shape=[shape, shape] if two else shape,
        compiler_params=_cparams("parallel"),
        name="ffn",
    )(x, g, w_in, w_out, g2)


def _norm_proj_kernel(x_ref, g_ref, w_ref, o_ref):
    h = _rms(x_ref[...], g_ref[...]).astype(BF16)
    o_ref[...] = _dot(h, w_ref[...])


def _norm_proj(x, g, w, *, tm=512):
    n, d = x.shape
    m = w.shape[1]
    return pl.pallas_call(
        _norm_proj_kernel,
        grid=(n // tm,),
        in_specs=[
            pl.BlockSpec((tm, d), lambda i: (i, 0)),
            pl.BlockSpec((1, d), lambda i: (0, 0)),
            pl.BlockSpec((d, m), lambda i: (0, 0)),
        ],
        out_specs=pl.BlockSpec((tm, m), lambda i: (i, 0)),
        out_shape=jax.ShapeDtypeStruct((n, m), F32),
        compiler_params=_cparams("parallel"),
        name="norm_proj",
    )(x, g, w)


def _attn_bias():
    w = WINDOW
    slopes = 2.0 ** (-8.0 * np.arange(1, A_HEADS + 1) / A_HEADS)
    dist = (np.arange(w) + w)[:, None] - np.arange(2 * w)[None, :]
    band = (dist >= 0) & (dist < w)
    has_prev = np.arange(2 * w)[None, :] >= w
    val = -slopes.reshape(A_KV_HEADS, A_GROUP, 1, 1) * dist.astype(np.float64)[None, None]
    out = np.stack([np.where(band & has_prev, val, -np.inf), np.where(band, val, -np.inf)])
    return jnp.asarray(out.reshape(2, A_KV_HEADS, A_GROUP * w, 2 * w), dtype=F32)


def _attn_kernel(sink_ref, bias_ref, q_ref, kc_ref, kp_ref, vc_ref, vp_ref, o_ref, *, nq):
    hk = pl.program_id(1)
    tile = pl.program_id(2)
    w = WINDOW
    dh = A_HEAD_DIM
    qw = A_GROUP * dh

    def widen(prev_ref, cur_ref):
        t = jnp.concatenate([prev_ref[...], cur_ref[...]], axis=0)
        r = pltpu.roll(t, dh, axis=1)
        lane = lax.broadcasted_iota(jnp.int32, t.shape, 1)
        t2 = jnp.where(lane // dh == hk, t, r)
        return jnp.concatenate([t2, t2], axis=1).astype(BF16)

    k4 = widen(kp_ref, kc_ref)
    v4 = widen(vp_ref, vc_ref)

    lane_head = lax.broadcasted_iota(jnp.int32, (w, qw), 1) // dh
    grp = lax.broadcasted_iota(jnp.int32, (A_GROUP * w, 1), 0) // w
    sink = jnp.zeros((A_GROUP * w, 1), F32)
    for g in range(A_GROUP):
        sink = jnp.where(grp == g, sink_ref[0, hk * A_GROUP + g], sink)

    bias_rest = bias_ref[1, 0]
    for blk in range(nq):
        q = q_ref[blk * w:(blk + 1) * w, :] * (dh ** -0.5)
        q4 = jnp.concatenate(
            [jnp.where(lane_head == g, q, 0.0) for g in range(A_GROUP)], axis=0).astype(BF16)
        keys = slice(blk * w, (blk + 2) * w)
        bias = jnp.where(tile == 0, bias_ref[0, 0], bias_rest) if blk == 0 else bias_rest
        s = _dot_nt(q4, k4[keys]) + bias
        m = jnp.maximum(jnp.max(s, axis=-1, keepdims=True), sink)
        p = jnp.exp(s - m)
        denom = jnp.sum(p, axis=-1, keepdims=True) + jnp.exp(sink - m)
        o4 = _dot(p.astype(BF16), v4[keys]) / denom
        o = jnp.zeros((w, qw), F32)
        for g in range(A_GROUP):
            o = o + jnp.where(lane_head == g, o4[g * w:(g + 1) * w, :], 0.0)
        o_ref[blk * w:(blk + 1) * w, :] = o.astype(o_ref.dtype)


def _attention(proj, sinks, batch, seq, *, nq=4):
    n = proj.shape[0]
    rows = nq * WINDOW
    nt = seq // rows
    qw = A_GROUP * A_HEAD_DIM
    kcol = A_Q_W // LANES
    vcol = (A_Q_W + A_KV_W) // LANES
    cur = lambda b, h, i: b * nt + i
    prev = lambda b, h, i: b * nt * nq + jnp.maximum(i * nq - 1, 0)
    bias = _attn_bias()
    return pl.pallas_call(
        functools.partial(_attn_kernel, nq=nq),
        grid=(batch, A_KV_HEADS, nt),
        in_specs=[
            pl.BlockSpec(memory_space=pltpu.SMEM),
            pl.BlockSpec((2, 1) + bias.shape[2:], lambda b, h, i: (0, h, 0, 0)),
            pl.BlockSpec((rows, qw), lambda b, h, i: (cur(b, h, i), h)),
            pl.BlockSpec((rows, LANES), lambda b, h, i: (cur(b, h, i), kcol)),
            pl.BlockSpec((WINDOW, LANES), lambda b, h, i: (prev(b, h, i), kcol)),
            pl.BlockSpec((rows, LANES), lambda b, h, i: (cur(b, h, i), vcol)),
            pl.BlockSpec((WINDOW, LANES), lambda b, h, i: (prev(b, h, i), vcol)),
        ],
        out_specs=pl.BlockSpec((rows, qw), lambda b, h, i: (cur(b, h, i), h)),
        out_shape=jax.ShapeDtypeStruct((n, A_Q_W), BF16),
        compiler_params=_cparams("parallel", "parallel", "parallel"),
        name="swa",
    )(sinks.reshape(1, A_HEADS), bias, proj, proj, proj, proj, proj)


def _split3(x):
    hi = x.astype(BF16)
    r1 = x - hi.astype(F32)
    mid = r1.astype(BF16)
    lo = (r1 - mid.astype(F32)).astype(BF16)
    return hi, mid, lo


def _hgrn_kernel(lb_ref, q_ref, f_ref, i_ref, g_ref, o_ref, st_ref, *, ts):
    c = B_CHUNK
    nchunk = ts // c

    @pl.when(pl.program_id(2) == 0)
    def _():
        st_ref[...] = jnp.zeros_like(st_ref)

    lb = lb_ref[...]
    f = lb + (1.0 - lb) * _sigmoid(f_ref[...])
    k = 1.0 - f
    logf = jnp.log(f)

    sub = LANES
    r2 = lax.broadcasted_iota(jnp.int32, (sub, sub), 0)
    c2 = lax.broadcasted_iota(jnp.int32, (sub, sub), 1)
    causal = ((r2 // c) == (c2 // c)) & (c2 <= r2)

    tri = causal.astype(BF16)
    parts = _split3(logf)
    cum = jnp.concatenate(
        [sum(_dot(tri, p[s0:s0 + sub]) for p in parts) for s0 in range(0, ts, sub)], axis=0)
    tot_rows = [cum[(ch + 1) * c - 1:(ch + 1) * c, :] for ch in range(nchunk)]
    tot = jnp.concatenate([jnp.broadcast_to(t, (c, B_DIM)) for t in tot_rows], axis=0)
    dec_rows = [jnp.exp(t) for t in tot_rows]

    qd = _silu(q_ref[...]) * jnp.exp(cum)
    kinv = (k * jnp.exp(-cum)).astype(BF16)
    kend = (k * jnp.exp(tot - cum)).astype(BF16)
    qd16 = qd.astype(BF16)
    v32 = i_ref[...]
    v16 = v32.astype(BF16)

    v_t = v32.T
    per_sub = sub // c
    outs = []
    kv_ts = []
    for sidx in range(ts // sub):
        sl = slice(sidx * sub, (sidx + 1) * sub)
        sc = jnp.where(causal, _dot_nt(qd16[sl], kinv[sl]), 0.0)
        outs.append(_dot(sc.astype(BF16), v16[sl]))
        v_ts = v_t[:, sl]
        lhs = jnp.concatenate(
            [jnp.where(c2 // c == jj, v_ts, 0.0) for jj in range(per_sub)], axis=0).astype(BF16)
        kv_ts.append(_dot(lhs, kend[sl]))
    o_intra = jnp.concatenate(outs, axis=0)

    st = st_ref[...]
    inter = []
    for ch in range(nchunk):
        sl = slice(ch * c, (ch + 1) * c)
        inter.append(_dot_nt(qd16[sl], st.astype(BF16)))
        jj = ch % per_sub
        kv_t = kv_ts[ch // per_sub][jj * B_DIM:(jj + 1) * B_DIM]
        st = st * dec_rows[ch] + kv_t
    st_ref[...] = st
    o = o_intra + jnp.concatenate(inter, axis=0)

    o = o * lax.rsqrt(jnp.mean(o * o, axis=-1, keepdims=True) + EPS)
    o_ref[...] = (o * _silu(g_ref[...])).astype(o_ref.dtype)


def _hgrn(proj, lb, batch, seq, *, ts=1024):
    n = proj.shape[0]
    nt = seq // ts
    base = (A_Q_W + 2 * A_KV_W) // LANES
    nh = B_W // LANES

    def col(kind):
        return lambda b, h, t: (b * nt + t, base + kind * nh + h)

    return pl.pallas_call(
        functools.partial(_hgrn_kernel, ts=ts),
        grid=(batch, B_HEADS, nt),
        in_specs=[
            pl.BlockSpec((1, LANES), lambda b, h, t: (0, h)),
            pl.BlockSpec((ts, LANES), col(0)),
            pl.BlockSpec((ts, LANES), col(1)),
            pl.BlockSpec((ts, LANES), col(2)),
            pl.BlockSpec((ts, LANES), col(3)),
        ],
        out_specs=pl.BlockSpec((ts, LANES), lambda b, h, t: (b * nt + t, h)),
        out_shape=jax.ShapeDtypeStruct((n, B_W), BF16),
        scratch_shapes=[pltpu.VMEM((B_DIM, B_DIM), F32)],
        compiler_params=_cparams("parallel", "parallel", "arbitrary"),
        name="hgrn2",
    )(lb.reshape(1, B_W), proj, proj, proj, proj)


def _mix_out_kernel(x_ref, a_ref, b_ref, wa_ref, wb_ref, o_ref):
    o_ref[...] = x_ref[...] + _dot(a_ref[...], wa_ref[...]) + _dot(b_ref[...], wb_ref[...])


def _mix_out(x, a, b, w, *, tm=1024):
    n, d = x.shape
    ka, kb = a.shape[1], b.shape[1]
    return pl.pallas_call(
        _mix_out_kernel,
        grid=(n // tm,),
        in_specs=[
            pl.BlockSpec((tm, d), lambda i: (i, 0)),
            pl.BlockSpec((tm, ka), lambda i: (i, 0)),
            pl.BlockSpec((tm, kb), lambda i: (i, 0)),
            pl.BlockSpec((ka, d), lambda i: (0, 0)),
            pl.BlockSpec((kb, d), lambda i: (1, 0)),
        ],
        out_specs=pl.BlockSpec((tm, d), lambda i: (i, 0)),
        out_shape=jax.ShapeDtypeStruct((n, d), F32),
        compiler_params=_cparams("parallel"),
        name="mix_out",
    )(x, a, b, w, w)


def _s5_rows(h_ref, nc):
    return jnp.concatenate(
        [h_ref[pl.ds(t, nc, stride=S5_T), :] for t in range(S5_T)], axis=1).astype(BF16)


S5_GPB = LANES // S5_GROUP


def _s5_sum_kernel(h_ref, w_ref, o_ref, wx_ref, *, nc):
    k = S5_T * LANES

    @pl.when(pl.program_id(1) == 0)
    def _():
        w = w_ref[0]
        row_group = (lax.broadcasted_iota(jnp.int32, (k, LANES), 0) % LANES) // S5_GROUP
        for g in range(S5_GPB):
            wx_ref[:, g * LANES:(g + 1) * LANES] = jnp.where(row_group == g, w, jnp.zeros_like(w))

    s = _dot(_s5_rows(h_ref, nc), wx_ref[...])
    for g in range(S5_GPB):
        o_ref[:, g, :] = s[:, g * LANES:(g + 1) * LANES]


def _s5_sum(h, w, batch, seq):
    nc = seq // S5_T
    nblk, k, _ = w.shape
    return pl.pallas_call(
        functools.partial(_s5_sum_kernel, nc=nc),
        grid=(nblk, batch),
        in_specs=[pl.BlockSpec((seq, LANES), lambda j, b: (b, j)),
                  pl.BlockSpec((1, k, LANES), lambda j, b: (j, 0, 0))],
        out_specs=pl.BlockSpec((nc, S5_GPB, LANES), lambda j, b: (b, j, 0)),
        out_shape=jax.ShapeDtypeStruct((batch * nc, S5_GROUPS, LANES), F32),
        scratch_shapes=[pltpu.VMEM((k, S5_GPB * LANES), BF16)],
        compiler_params=_cparams("parallel", "arbitrary"),
        name="s5_chunk_sum",
    )(h, w)


def _s5_scan_kernel(s_ref, a1_ref, a2_ref, o_ref, x_ref, *, tc, nb):
    @pl.when(pl.program_id(0) == 0)
    def _():
        x_ref[...] = jnp.zeros_like(x_ref)

    a1 = a1_ref[...]
    a2 = a2_ref[...]

    def step(c, xs):
        new = []
        for b in range(nb):
            o_ref[b, c] = xs[b]
            new.append(a1 * xs[b] + a2 * pltpu.roll(xs[b], S5_STATE, axis=1) + s_ref[b, c])
        return tuple(new)

    xs = lax.fori_loop(0, tc, step, tuple(x_ref[b] for b in range(nb)))
    for b in range(nb):
        x_ref[b] = xs[b]


def _s5_scan(s, a1, a2, *, tc=32):
    nb, nc, rows, lanes = s.shape
    return pl.pallas_call(
        functools.partial(_s5_scan_kernel, tc=tc, nb=nb),
        grid=(nc // tc,),
        in_specs=[pl.BlockSpec((nb, tc, rows, lanes), lambda i: (0, i, 0, 0)),
                  pl.BlockSpec((rows, lanes), lambda i: (0, 0)),
                  pl.BlockSpec((rows, lanes), lambda i: (0, 0))],
        out_specs=pl.BlockSpec((nb, tc, rows, lanes), lambda i: (0, i, 0, 0)),
        out_shape=jax.ShapeDtypeStruct(s.shape, F32),
        scratch_shapes=[pltpu.VMEM((nb, rows, lanes), F32)],
        compiler_params=_cparams("arbitrary"),
        name="s5_chunk_scan",
    )(s, a1, a2)


def _s5_out_kernel(h_ref, x_ref, lag_ref, wc_ref, o_ref, wt_ref, *, nc):
    t = S5_T
    half_t = t // 2
    half = half_t * LANES

    @pl.when(pl.program_id(1) == 0)
    def _():
        zero = jnp.zeros((LANES, LANES), BF16)
        for s in range(t):
            for tp in range(half_t if s >= half_t else 0, t):
                blk = lag_ref[0, tp - s] if tp >= s else zero
                wt_ref[s * LANES:(s + 1) * LANES, tp * LANES:(tp + 1) * LANES] = blk

    a = _s5_rows(h_ref, nc)
    xp = jnp.concatenate([x_ref[:, g, :] for g in range(S5_GPB)], axis=1).astype(BF16)
    y0 = _dot(a[:, :half], wt_ref[:half, :half]) + _dot(xp, wc_ref[0, :, :half])
    y1 = _dot(a, wt_ref[:, half:]) + _dot(xp, wc_ref[0, :, half:])
    for tt in range(half_t):
        o_ref[pl.ds(tt, nc, stride=t), :] = y0[:, tt * LANES:(tt + 1) * LANES]
        o_ref[pl.ds(half_t + tt, nc, stride=t), :] = y1[:, tt * LANES:(tt + 1) * LANES]


def _s5_out(h, xprev, w_lag, w_carry, batch, seq):
    nc = seq // S5_T
    nblk = w_lag.shape[0]
    kc = w_carry.shape[1]
    k = S5_T * LANES
    return pl.pallas_call(
        functools.partial(_s5_out_kernel, nc=nc),
        grid=(nblk, batch),
        in_specs=[pl.BlockSpec((seq, LANES), lambda j, b: (b, j)),
                  pl.BlockSpec((nc, S5_GPB, LANES), lambda j, b: (b, j, 0)),
                  pl.BlockSpec((1,) + w_lag.shape[1:], lambda j, b: (j, 0, 0, 0)),
                  pl.BlockSpec((1, kc, k), lambda j, b: (j, 0, 0))],
        out_specs=pl.BlockSpec((seq, LANES), lambda j, b: (b, j)),
        out_shape=jax.ShapeDtypeStruct(h.shape, F32),
        scratch_shapes=[pltpu.VMEM((k, k), BF16)],
        compiler_params=_cparams("parallel", "arbitrary"),
        name="s5_chunk_out",
    )(h, xprev, w_lag, w_carry)


def _s5_glu_kernel(x_ref, h_ref, y_ref, d_ref, w_ref, o_ref):
    y = y_ref[...] + d_ref[...] * h_ref[...]
    inner = math.sqrt(2.0 / math.pi) * (y + 0.044715 * (y * y * y))
    ge = y * (0.5 * (1.0 + jnp.tanh(inner)))
    z = _dot(ge.astype(BF16), w_ref[...])
    d = x_ref.shape[1]
    o_ref[...] = x_ref[...] + z[:, :d] * _sigmoid(z[:, d:])


def _s5_glu(x, h, y, dskip, w, *, tm=512):
    n, d = x.shape
    tile = pl.BlockSpec((tm, d), lambda i: (i, 0))
    return pl.pallas_call(
        _s5_glu_kernel,
        grid=(n // tm,),
        in_specs=[tile, tile, tile,
                  pl.BlockSpec((1, d), lambda i: (0, 0)),
                  pl.BlockSpec((d, 2 * d), lambda i: (0, 0))],
        out_specs=tile,
        out_shape=jax.ShapeDtypeStruct((n, d), F32),
        compiler_params=_cparams("parallel"),
        name="s5_glu",
    )(x, h, y, dskip, w)


def _s5_tables(a_re, a_im, log_step, b_re, b_im, c_re, c_im):
    hp = lax.Precision.HIGHEST
    t = S5_T
    step = jnp.exp(log_step)[:, None]
    mag = jnp.exp(step * a_re)
    ab_r = mag * jnp.cos(step * a_im)
    ab_i = mag * jnp.sin(step * a_im)
    den = a_re * a_re + a_im * a_im
    coef_r = ((ab_r - 1.0) * a_re + ab_i * a_im) / den
    coef_i = (ab_i * a_re - (ab_r - 1.0) * a_im) / den
    bb_r = coef_r[..., None] * b_re - coef_i[..., None] * b_im
    bb_i = coef_r[..., None] * b_im + coef_i[..., None] * b_re
    j = jnp.arange(t + 1, dtype=F32)[:, None, None]
    pmag = jnp.exp(j * (step * a_re)[None])
    pw_r = pmag * jnp.cos(j * (step * a_im)[None])
    pw_i = pmag * jnp.sin(j * (step * a_im)[None])

    rev_r, rev_i = pw_r[t - 1::-1], pw_i[t - 1::-1]
    ws_r = rev_r[:, :, :, None] * bb_r[None] - rev_i[:, :, :, None] * bb_i[None]
    ws_i = rev_r[:, :, :, None] * bb_i[None] + rev_i[:, :, :, None] * bb_r[None]
    w_sum = jnp.concatenate([ws_r, ws_i], axis=2)
    w_sum = w_sum.transpose(1, 0, 3, 2).reshape(S5_GROUPS, S5_ROW, 2 * S5_STATE)

    lb_r = pw_r[:t, :, :, None] * bb_r[None] - pw_i[:t, :, :, None] * bb_i[None]
    lb_i = pw_r[:t, :, :, None] * bb_i[None] + pw_i[:t, :, :, None] * bb_r[None]
    kmat = (jnp.einsum('gdp,jgpc->jgdc', c_re, lb_r, precision=hp)
            - jnp.einsum('gdp,jgpc->jgdc', c_im, lb_i, precision=hp))

    q_r, q_i = pw_r[1:], pw_i[1:]
    wc_r = c_re[None] * q_r[:, :, None, :] - c_im[None] * q_i[:, :, None, :]
    wc_i = -(c_re[None] * q_i[:, :, None, :] + c_im[None] * q_r[:, :, None, :])
    w_carry = jnp.concatenate([wc_r, wc_i], axis=3)
    w_carry = w_carry.transpose(1, 3, 0, 2).reshape(S5_GROUPS, 2 * S5_STATE, S5_ROW)

    a1 = jnp.concatenate([pw_r[t], pw_r[t]], axis=1)
    a2 = jnp.concatenate([-pw_i[t], pw_i[t]], axis=1)

    gl = S5_GPB
    nblk = S5_GROUPS // gl
    ws = w_sum.astype(BF16).reshape(nblk, gl, t, S5_GROUP, 2 * S5_STATE).transpose(0, 2, 1, 3, 4)
    ws = ws.reshape(nblk, t * LANES, 2 * S5_STATE)

    def block_diag(tbl, rows_per_group, reps):
        col = jnp.arange(reps * LANES)
        src = jnp.arange(reps * S5_GROUP)
        spread = ((col[None, :] // LANES == src[:, None] // S5_GROUP)
                  & (col[None, :] % S5_GROUP == src[:, None] % S5_GROUP)).astype(BF16)
        wide = jnp.einsum('...rc,cl->...rl', tbl.astype(BF16), spread, preferred_element_type=F32)
        row_group = (jnp.arange(tbl.shape[-2]) // rows_per_group) % gl
        col_group = (col % LANES) // S5_GROUP
        return jnp.where(row_group[:, None] == col_group[None, :], wide, 0.0).astype(BF16)

    lag = block_diag(kmat.transpose(0, 1, 3, 2).reshape(t, S5_GROUPS * S5_GROUP, S5_GROUP), S5_GROUP, 1)
    lag = lag.reshape(t, nblk, LANES, LANES).transpose(1, 0, 2, 3)
    wc = block_diag(w_carry.reshape(S5_GROUPS * 2 * S5_STATE, S5_ROW), 2 * S5_STATE, t)
    wc = wc.reshape(nblk, gl * 2 * S5_STATE, t * LANES)
    return ws, lag, wc, a1, a2


def _s5_mixer(x, h, tables, dskip, w_glu, batch, seq):
    w_sum, w_lag, w_carry, a1, a2 = tables
    nc = seq // S5_T
    s = _s5_sum(h, w_sum, batch, seq)
    s4 = s.reshape(batch, nc, S5_GROUPS, 2 * S5_STATE)
    xprev = _s5_scan(s4, a1, a2).reshape(s.shape)
    y = _s5_out(h, xprev, w_lag, w_carry, batch, seq)
    return _s5_glu(x, h, y, dskip, w_glu)


def kernel(x, norm_g, ffn_w_in, ffn_w_out, mix_w_in, attn_sinks, hgrn_lb, mix_w_out, s5_a_re, s5_a_im, s5_log_step, s5_b_re, s5_b_im, s5_c_re, s5_c_im, s5_d, s5_w_glu, final_g):
    batch, seq, d = x.shape
    n = batch * seq
    x = x.reshape(n, d)

    w_in16 = ffn_w_in.astype(BF16)
    w_out16 = ffn_w_out.astype(BF16)
    mix_in16 = mix_w_in.astype(BF16)
    mix_out16 = mix_w_out.astype(BF16)
    glu16 = s5_w_glu.astype(BF16)
    fg = final_g.reshape(1, d)

    lb_p = jax.nn.softmax(hgrn_lb.astype(F32), axis=0)
    lb_all = jnp.cumsum(lb_p, axis=0) - lb_p[0]

    for layer in range(DEPTH):
        g3 = norm_g[layer].reshape(3, 1, d)
        if layer % 2 == 0:
            e = layer // 2
            x = _ffn(x, g3[0], w_in16, w_out16, layer, 0, g3[1])
            proj = _norm_proj(x, g3[1], mix_in16[e])
            attn = _attention(proj, attn_sinks[e], batch, seq)
            rec = _hgrn(proj, lb_all[e], batch, seq)
            x = _mix_out(x, attn, rec, mix_out16[e])
        else:
            o = layer // 2
            tables = _s5_tables(s5_a_re[o], s5_a_im[o], s5_log_step[o], s5_b_re[o], s5_b_im[o],
                                s5_c_re[o], s5_c_im[o])
            x, h = _ffn(x, g3[0], w_in16, w_out16, layer, 0, g3[1], post="also_norm")
            x = _s5_mixer(x, h, tables, s5_d[o].reshape(1, d), glu16[o], batch, seq)
        last = layer == DEPTH - 1
        x = _ffn(x, g3[2], w_in16, w_out16, layer, 1, fg, post="norm_only" if last else "none")
    return x.reshape(batch, seq, d)
```

```python
import functools
import math

import jax
import jax.numpy as jnp
import numpy as np
from jax import lax
from jax.experimental import pallas as pl
from jax.experimental.pallas import tpu as pltpu

F32 = jnp.float32
BF16 = jnp.bfloat16

EPS = 1e-6
DEPTH = 4
D_MODEL = 1024
D_FF = 2816
A_HEADS = 8
A_KV_HEADS = 2
A_GROUP = A_HEADS // A_KV_HEADS
A_HEAD_DIM = 64
WINDOW = 128
A_Q_W = A_HEADS * A_HEAD_DIM
A_KV_W = A_KV_HEADS * A_HEAD_DIM
B_HEADS = 4
B_DIM = 128
B_CHUNK = 32
B_W = B_HEADS * B_DIM
MIX_IN = A_Q_W + 2 * A_KV_W + 4 * B_W
S5_GROUP = 16
S5_GROUPS = D_MODEL // S5_GROUP
S5_STATE = 64
S5_T = 16
S5_ROW = S5_T * S5_GROUP

LANES = 128
VMEM_LIMIT = 56 * 1024 * 1024


def _cparams(*sem):
    return pltpu.CompilerParams(dimension_semantics=sem, vmem_limit_bytes=VMEM_LIMIT)


def _sigmoid(x):
    return 1.0 / (1.0 + jnp.exp(-x))


def _silu(x):
    return x * _sigmoid(x)


def _rms(x, g):
    return x * lax.rsqrt(jnp.mean(x * x, axis=-1, keepdims=True) + EPS) * g


def _dot(a, b):
    return jnp.dot(a, b, preferred_element_type=F32)


def _dot_nt(a, b):
    return lax.dot_general(a, b, (((1,), (1,)), ((), ())), preferred_element_type=F32)


def _mix_residual(x_ref, a_ref, b_ref, w_ref):
    ka = a_ref.shape[1]
    return x_ref[...] + _dot(a_ref[...], w_ref[:ka, :]) + _dot(b_ref[...], w_ref[ka:, :])


def _s5_residual(x_ref, h_ref, y_ref, d_ref, w_ref):
    y = y_ref[...] + d_ref[...] * h_ref[...]
    inner = math.sqrt(2.0 / math.pi) * (y + 0.044715 * (y * y * y))
    ge = y * (0.5 * (1.0 + jnp.tanh(inner)))
    z = _dot(ge.astype(BF16), w_ref[...])
    d = x_ref.shape[1]
    return x_ref[...] + z[:, :d] * _sigmoid(z[:, d:])


_FFN_PRE = {"none": (1, lambda x_ref: x_ref[...]), "mix": (4, _mix_residual), "s5": (5, _s5_residual)}


def _ffn_kernel(*refs, tf, pre, post):
    n_pre, pre_fn = _FFN_PRE[pre]
    g_ref, wi_ref, wo_ref, g2_ref = refs[n_pre:n_pre + 4]
    o_refs = refs[n_pre + 4:]
    x = pre_fn(*refs[:n_pre])
    h = _rms(x, g_ref[...]).astype(BF16)
    acc = None
    for c in range(D_FF // tf):
        gate = _dot(h, wi_ref[:, c * tf:(c + 1) * tf])
        up = _dot(h, wi_ref[:, D_FF + c * tf:D_FF + (c + 1) * tf])
        act = (_silu(gate) * up).astype(BF16)
        part = _dot(act, wo_ref[c * tf:(c + 1) * tf, :])
        acc = part if acc is None else acc + part
    y = x + 0.5 * acc
    if post == "norm_only":
        o_refs[0][...] = _rms(y, g2_ref[...])
    else:
        o_refs[0][...] = y
        if post == "also_norm":
            o_refs[1][...] = _rms(y, g2_ref[...])


def _resident(shape, *index):
    lead = len(index)
    block = (None,) * lead + tuple(shape)
    full = tuple(index) + (0,) * len(shape)
    return pl.BlockSpec(block, lambda *_: full, pipeline_mode=pl.Buffered(1))


def _ffn(x, g, w_in, w_out, layer, which, g2, *, pre="none", pre_args=(), post="none", tm=512, tf=256):
    n, d = x.shape
    rows = lambda a: pl.BlockSpec((tm, a.shape[1]), lambda i: (i, 0))
    tile = pl.BlockSpec((tm, d), lambda i: (i, 0))
    if pre == "mix":
        a, b, w, idx = pre_args
        pre_ops, pre_specs = (a, b, w), [rows(a), rows(b), _resident(w.shape[1:], idx)]
    elif pre == "s5":
        h, y, dskip, w, idx = pre_args
        pre_ops = (h, y, dskip, w)
        pre_specs = [tile, tile, _resident(dskip.shape), _resident(w.shape[1:], idx)]
    else:
        pre_ops, pre_specs = (), []
    shape = jax.ShapeDtypeStruct((n, d), F32)
    two = post == "also_norm"
    return pl.pallas_call(
        functools.partial(_ffn_kernel, tf=tf, pre=pre, post=post),
        grid=(n // tm,),
        in_specs=[tile] + pre_specs + [
            _resident((1, d)),
            _resident((d, 2 * D_FF), layer, which),
            _resident((D_FF, d), layer, which),
            _resident((1, d)),
        ],
        out_specs=[tile, tile] if two else tile,
        out_shape=[shape, shape] if two else shape,
        compiler_params=_cparams("parallel"),
        name="ffn",
    )(x, *pre_ops, g, w_in, w_out, g2)


def _norm_proj_kernel(x_ref, g_ref, w_ref, o_ref):
    h = _rms(x_ref[...], g_ref[...]).astype(BF16)
    o_ref[...] = _dot(h, w_ref[...])


def _norm_proj(x, g, w, *, tm=512):
    n, d = x.shape
    m = w.shape[1]
    return pl.pallas_call(
        _norm_proj_kernel,
        grid=(n // tm,),
        in_specs=[
            pl.BlockSpec((tm, d), lambda i: (i, 0)),
            pl.BlockSpec((1, d), lambda i: (0, 0)),
            pl.BlockSpec((d, m), lambda i: (0, 0)),
        ],
        out_specs=pl.BlockSpec((tm, m), lambda i: (i, 0)),
        out_shape=jax.ShapeDtypeStruct((n, m), F32),
        compiler_params=_cparams("parallel"),
        name="norm_proj",
    )(x, g, w)


def _attn_bias():
    w = WINDOW
    slopes = 2.0 ** (-8.0 * np.arange(1, A_HEADS + 1) / A_HEADS)
    dist = (np.arange(w) + w)[:, None] - np.arange(2 * w)[None, :]
    band = (dist >= 0) & (dist < w)
    has_prev = np.arange(2 * w)[None, :] >= w
    val = -slopes.reshape(A_KV_HEADS, A_GROUP, 1, 1) * dist.astype(np.float64)[None, None]
    out = np.stack([np.where(band & has_prev, val, -np.inf), np.where(band, val, -np.inf)])
    return jnp.asarray(out.reshape(2, A_KV_HEADS, A_GROUP * w, 2 * w), dtype=F32)


def _attn_kernel(sink_ref, bias_ref, q_ref, kc_ref, kp_ref, vc_ref, vp_ref, o_ref, *, nq):
    hk = pl.program_id(1)
    tile = pl.program_id(2)
    w = WINDOW
    dh = A_HEAD_DIM
    qw = A_GROUP * dh

    def widen(prev_ref, cur_ref):
        t = jnp.concatenate([prev_ref[...], cur_ref[...]], axis=0)
        r = pltpu.roll(t, dh, axis=1)
        lane = lax.broadcasted_iota(jnp.int32, t.shape, 1)
        t2 = jnp.where(lane // dh == hk, t, r)
        return jnp.concatenate([t2, t2], axis=1).astype(BF16)

    k4 = widen(kp_ref, kc_ref)
    v4 = widen(vp_ref, vc_ref)

    lane_head = lax.broadcasted_iota(jnp.int32, (w, qw), 1) // dh
    grp = lax.broadcasted_iota(jnp.int32, (A_GROUP * w, 1), 0) // w
    sink = jnp.zeros((A_GROUP * w, 1), F32)
    for g in range(A_GROUP):
        sink = jnp.where(grp == g, sink_ref[0, hk * A_GROUP + g], sink)

    bias_rest = bias_ref[1, 0]
    blocks = range(nq)
    keys = [slice(blk * w, (blk + 2) * w) for blk in blocks]

    scores = []
    for blk in blocks:
        q = q_ref[blk * w:(blk + 1) * w, :] * (dh ** -0.5)
        q4 = jnp.concatenate(
            [jnp.where(lane_head == g, q, 0.0) for g in range(A_GROUP)], axis=0).astype(BF16)
        bias = jnp.where(tile == 0, bias_ref[0, 0], bias_rest) if blk == 0 else bias_rest
        scores.append(_dot_nt(q4, k4[keys[blk]]) + bias)

    probs, denoms = [], []
    for s in scores:
        m = jnp.maximum(jnp.max(s, axis=-1, keepdims=True), sink)
        p = jnp.exp(s - m)
        denoms.append(jnp.sum(p, axis=-1, keepdims=True) + jnp.exp(sink - m))
        probs.append(p.astype(BF16))

    outs = [_dot(probs[blk], v4[keys[blk]]) / denoms[blk] for blk in blocks]

    for blk in blocks:
        o = jnp.zeros((w, qw), F32)
        for g in range(A_GROUP):
            o = o + jnp.where(lane_head == g, outs[blk][g * w:(g + 1) * w, :], 0.0)
        o_ref[blk * w:(blk + 1) * w, :] = o.astype(o_ref.dtype)


def _attention(proj, sinks, batch, seq, *, nq=4):
    n = proj.shape[0]
    rows = nq * WINDOW
    nt = seq // rows
    qw = A_GROUP * A_HEAD_DIM
    kcol = A_Q_W // LANES
    vcol = (A_Q_W + A_KV_W) // LANES
    cur = lambda b, h, i: b * nt + i
    prev = lambda b, h, i: b * nt * nq + jnp.maximum(i * nq - 1, 0)
    bias = _attn_bias()
    return pl.pallas_call(
        functools.partial(_attn_kernel, nq=nq),
        grid=(batch, A_KV_HEADS, nt),
        in_specs=[
            pl.BlockSpec(memory_space=pltpu.SMEM),
            pl.BlockSpec((2, 1) + bias.shape[2:], lambda b, h, i: (0, h, 0, 0)),
            pl.BlockSpec((rows, qw), lambda b, h, i: (cur(b, h, i), h)),
            pl.BlockSpec((rows, LANES), lambda b, h, i: (cur(b, h, i), kcol)),
            pl.BlockSpec((WINDOW, LANES), lambda b, h, i: (prev(b, h, i), kcol)),
            pl.BlockSpec((rows, LANES), lambda b, h, i: (cur(b, h, i), vcol)),
            pl.BlockSpec((WINDOW, LANES), lambda b, h, i: (prev(b, h, i), vcol)),
        ],
        out_specs=pl.BlockSpec((rows, qw), lambda b, h, i: (cur(b, h, i), h)),
        out_shape=jax.ShapeDtypeStruct((n, A_Q_W), BF16),
        compiler_params=_cparams("parallel", "parallel", "parallel"),
        name="swa",
    )(sinks.reshape(1, A_HEADS), bias, proj, proj, proj, proj, proj)


def _split3(x):
    hi = x.astype(BF16)
    r1 = x - hi.astype(F32)
    mid = r1.astype(BF16)
    lo = (r1 - mid.astype(F32)).astype(BF16)
    return hi, mid, lo


def _hgrn_kernel(lb_ref, q_ref, f_ref, i_ref, g_ref, o_ref, st_ref, *, ts, heads):
    @pl.when(pl.program_id(2) == 0)
    def _():
        st_ref[...] = jnp.zeros_like(st_ref)

    lanes = [slice(hd * B_DIM, (hd + 1) * B_DIM) for hd in range(heads)]
    args = [(lb_ref[:, ln], q_ref[:, ln], f_ref[:, ln], i_ref[:, ln], g_ref[:, ln], st_ref[hd])
            for hd, ln in enumerate(lanes)]
    results = [_hgrn_head(*a, ts) for a in args]
    for hd, (out, st) in enumerate(results):
        st_ref[hd] = st
        o_ref[:, lanes[hd]] = out.astype(o_ref.dtype)


def _hgrn_head(lb, q, f_logit, v32, g, st, ts):
    c = B_CHUNK
    nchunk = ts // c
    f = lb + (1.0 - lb) * _sigmoid(f_logit)
    k = 1.0 - f
    logf = jnp.log(f)

    sub = LANES
    r2 = lax.broadcasted_iota(jnp.int32, (sub, sub), 0)
    c2 = lax.broadcasted_iota(jnp.int32, (sub, sub), 1)
    causal = ((r2 // c) == (c2 // c)) & (c2 <= r2)

    tri = causal.astype(BF16)
    parts = _split3(logf)
    cum = jnp.concatenate(
        [sum(_dot(tri, p[s0:s0 + sub]) for p in parts) for s0 in range(0, ts, sub)], axis=0)
    tot_rows = [cum[(ch + 1) * c - 1:(ch + 1) * c, :] for ch in range(nchunk)]
    tot = jnp.concatenate([jnp.broadcast_to(t, (c, B_DIM)) for t in tot_rows], axis=0)
    dec_rows = [jnp.exp(t) for t in tot_rows]

    qd = _silu(q) * jnp.exp(cum)
    kinv = (k * jnp.exp(-cum)).astype(BF16)
    kend = k * jnp.exp(tot - cum)
    v16 = v32.astype(BF16)
    v_t = v32.T.astype(BF16)

    per = sub // c
    chunk_gap = r2 // c - c2 // c
    one = jnp.ones((1, B_DIM), F32)

    def by_chunk(rows):
        return jnp.concatenate([jnp.broadcast_to(x, (c, B_DIM)) for x in rows], axis=0)

    groups = range(ts // sub)
    rows = [slice(s * sub, (s + 1) * sub) for s in groups]
    decs = [dec_rows[s * per:(s + 1) * per] for s in groups]

    states = []
    for s in groups:
        d = decs[s]
        states.append(st.astype(BF16))
        to_end = by_chunk([d[1] * d[2] * d[3], d[2] * d[3], d[3], one])
        kv_t = _dot(v_t[:, rows[s]], (kend[rows[s]] * to_end).astype(BF16))
        st = st * (d[0] * d[1] * d[2] * d[3]) + kv_t

    scores = []
    for s in groups:
        d = decs[s]
        between1 = by_chunk([one, one, d[1], d[2]])
        between2 = by_chunk([one, one, one, d[1] * d[2]])
        qs = qd[rows[s]]
        same = _dot_nt(qs.astype(BF16), kinv[rows[s]])
        lhs = jnp.concatenate([qs, qs * between1, qs * between2], axis=0).astype(BF16)
        cross = _dot_nt(lhs, kend[rows[s]].astype(BF16))
        sc = jnp.where(causal, same,
                       jnp.where(chunk_gap == 1, cross[:sub],
                                 jnp.where(chunk_gap == 2, cross[sub:2 * sub],
                                           jnp.where(chunk_gap == 3, cross[2 * sub:], 0.0))))
        scores.append(sc.astype(BF16))

    outs = []
    for s in groups:
        d = decs[s]
        from_start = by_chunk([one, d[0], d[0] * d[1], d[0] * d[1] * d[2]])
        outs.append(_dot(scores[s], v16[rows[s]])
                    + _dot_nt((qd[rows[s]] * from_start).astype(BF16), states[s]))
    o = jnp.concatenate(outs, axis=0)

    o = o * lax.rsqrt(jnp.mean(o * o, axis=-1, keepdims=True) + EPS)
    return o * _silu(g), st


def _hgrn(proj, lb, batch, seq, *, ts=1024, heads=1):
    n = proj.shape[0]
    nt = seq // ts
    w = heads * B_DIM
    assert (A_Q_W + 2 * A_KV_W) % w == 0 and B_W % w == 0
    base = (A_Q_W + 2 * A_KV_W) // w
    ng = B_W // w

    def col(kind):
        return lambda b, h, t: (b * nt + t, base + kind * ng + h)

    return pl.pallas_call(
        functools.partial(_hgrn_kernel, ts=ts, heads=heads),
        grid=(batch, ng, nt),
        in_specs=[
            pl.BlockSpec((1, w), lambda b, h, t: (0, h)),
            pl.BlockSpec((ts, w), col(0)),
            pl.BlockSpec((ts, w), col(1)),
            pl.BlockSpec((ts, w), col(2)),
            pl.BlockSpec((ts, w), col(3)),
        ],
        out_specs=pl.BlockSpec((ts, w), lambda b, h, t: (b * nt + t, h)),
        out_shape=jax.ShapeDtypeStruct((n, B_W), BF16),
        scratch_shapes=[pltpu.VMEM((heads, B_DIM, B_DIM), F32)],
        compiler_params=_cparams("parallel", "parallel", "arbitrary"),
        name="hgrn2",
    )(lb.reshape(1, B_W), proj, proj, proj, proj)


def _s5_rows(h_ref, nc):
    return jnp.concatenate(
        [h_ref[pl.ds(t, nc, stride=S5_T), :] for t in range(S5_T)], axis=1).astype(BF16)


S5_GPB = LANES // S5_GROUP


def _s5_sum_kernel(h_ref, w_ref, o_ref, wx_ref, *, nc):
    k = S5_T * LANES

    @pl.when(pl.program_id(1) == 0)
    def _():
        w = w_ref[0]
        row_group = (lax.broadcasted_iota(jnp.int32, (k, LANES), 0) % LANES) // S5_GROUP
        for g in range(S5_GPB):
            wx_ref[:, g * LANES:(g + 1) * LANES] = jnp.where(row_group == g, w, jnp.zeros_like(w))

    s = _dot(_s5_rows(h_ref, nc), wx_ref[...])
    for g in range(S5_GPB):
        o_ref[:, g, :] = s[:, g * LANES:(g + 1) * LANES]


def _s5_sum(h, w, batch, seq):
    nc = seq // S5_T
    nblk, k, _ = w.shape
    return pl.pallas_call(
        functools.partial(_s5_sum_kernel, nc=nc),
        grid=(nblk, batch),
        in_specs=[pl.BlockSpec((seq, LANES), lambda j, b: (b, j)),
                  pl.BlockSpec((1, k, LANES), lambda j, b: (j, 0, 0))],
        out_specs=pl.BlockSpec((nc, S5_GPB, LANES), lambda j, b: (b, j, 0)),
        out_shape=jax.ShapeDtypeStruct((batch * nc, S5_GROUPS, LANES), F32),
        scratch_shapes=[pltpu.VMEM((k, S5_GPB * LANES), BF16)],
        compiler_params=_cparams("parallel", "arbitrary"),
        name="s5_chunk_sum",
    )(h, w)


def _s5_scan_kernel(s_ref, a1_ref, a2_ref, o_ref, x_ref, *, tc, nb):
    @pl.when(pl.program_id(0) == 0)
    def _():
        x_ref[...] = jnp.zeros_like(x_ref)

    a1 = a1_ref[...]
    a2 = a2_ref[...]

    def step(c, xs):
        new = []
        for b in range(nb):
            o_ref[b, c] = xs[b]
            new.append(a1 * xs[b] + a2 * pltpu.roll(xs[b], S5_STATE, axis=1) + s_ref[b, c])
        return tuple(new)

    xs = lax.fori_loop(0, tc, step, tuple(x_ref[b] for b in range(nb)))
    for b in range(nb):
        x_ref[b] = xs[b]


def _s5_scan(s, a1, a2, *, tc=32):
    nb, nc, rows, lanes = s.shape
    return pl.pallas_call(
        functools.partial(_s5_scan_kernel, tc=tc, nb=nb),
        grid=(nc // tc,),
        in_specs=[pl.BlockSpec((nb, tc, rows, lanes), lambda i: (0, i, 0, 0)),
                  pl.BlockSpec((rows, lanes), lambda i: (0, 0)),
                  pl.BlockSpec((rows, lanes), lambda i: (0, 0))],
        out_specs=pl.BlockSpec((nb, tc, rows, lanes), lambda i: (0, i, 0, 0)),
        out_shape=jax.ShapeDtypeStruct(s.shape, F32),
        scratch_shapes=[pltpu.VMEM((nb, rows, lanes), F32)],
        compiler_params=_cparams("arbitrary"),
        name="s5_chunk_scan",
    )(s, a1, a2)


def _s5_out_kernel(h_ref, x_ref, lag_ref, wc_ref, o_ref, wt_ref, *, nc):
    t = S5_T
    half_t = t // 2
    half = half_t * LANES

    @pl.when(pl.program_id(1) == 0)
    def _():
        zero = jnp.zeros((LANES, LANES), BF16)
        for s in range(t):
            for tp in range(half_t if s >= half_t else 0, t):
                blk = lag_ref[0, tp - s] if tp >= s else zero
                wt_ref[s * LANES:(s + 1) * LANES, tp * LANES:(tp + 1) * LANES] = blk

    a = _s5_rows(h_ref, nc)
    xp = jnp.concatenate([x_ref[:, g, :] for g in range(S5_GPB)], axis=1).astype(BF16)
    y0 = _dot(a[:, :half], wt_ref[:half, :half]) + _dot(xp, wc_ref[0, :, :half])
    y1 = _dot(a, wt_ref[:, half:]) + _dot(xp, wc_ref[0, :, half:])
    for tt in range(half_t):
        o_ref[pl.ds(tt, nc, stride=t), :] = y0[:, tt * LANES:(tt + 1) * LANES]
        o_ref[pl.ds(half_t + tt, nc, stride=t), :] = y1[:, tt * LANES:(tt + 1) * LANES]


def _s5_out(h, xprev, w_lag, w_carry, batch, seq):
    nc = seq // S5_T
    nblk = w_lag.shape[0]
    kc = w_carry.shape[1]
    k = S5_T * LANES
    return pl.pallas_call(
        functools.partial(_s5_out_kernel, nc=nc),
        grid=(nblk, batch),
        in_specs=[pl.BlockSpec((seq, LANES), lambda j, b: (b, j)),
                  pl.BlockSpec((nc, S5_GPB, LANES), lambda j, b: (b, j, 0)),
                  pl.BlockSpec((1,) + w_lag.shape[1:], lambda j, b: (j, 0, 0, 0)),
                  pl.BlockSpec((1, kc, k), lambda j, b: (j, 0, 0))],
        out_specs=pl.BlockSpec((seq, LANES), lambda j, b: (b, j)),
        out_shape=jax.ShapeDtypeStruct(h.shape, F32),
        scratch_shapes=[pltpu.VMEM((k, k), BF16)],
        compiler_params=_cparams("parallel", "arbitrary"),
        name="s5_chunk_out",
    )(h, xprev, w_lag, w_carry)


def _s5_tables(a_re, a_im, log_step, b_re, b_im, c_re, c_im):
    hp = lax.Precision.HIGHEST
    t = S5_T
    step = jnp.exp(log_step)[:, None]
    mag = jnp.exp(step * a_re)
    ab_r = mag * jnp.cos(step * a_im)
    ab_i = mag * jnp.sin(step * a_im)
    den = a_re * a_re + a_im * a_im
    coef_r = ((ab_r - 1.0) * a_re + ab_i * a_im) / den
    coef_i = (ab_i * a_re - (ab_r - 1.0) * a_im) / den
    bb_r = coef_r[..., None] * b_re - coef_i[..., None] * b_im
    bb_i = coef_r[..., None] * b_im + coef_i[..., None] * b_re
    j = jnp.arange(t + 1, dtype=F32)[:, None, None]
    pmag = jnp.exp(j * (step * a_re)[None])
    pw_r = pmag * jnp.cos(j * (step * a_im)[None])
    pw_i = pmag * jnp.sin(j * (step * a_im)[None])

    rev_r, rev_i = pw_r[t - 1::-1], pw_i[t - 1::-1]
    ws_r = rev_r[:, :, :, None] * bb_r[None] - rev_i[:, :, :, None] * bb_i[None]
    ws_i = rev_r[:, :, :, None] * bb_i[None] + rev_i[:, :, :, None] * bb_r[None]
    w_sum = jnp.concatenate([ws_r, ws_i], axis=2)
    w_sum = w_sum.transpose(1, 0, 3, 2).reshape(S5_GROUPS, S5_ROW, 2 * S5_STATE)

    lb_r = pw_r[:t, :, :, None] * bb_r[None] - pw_i[:t, :, :, None] * bb_i[None]
    lb_i = pw_r[:t, :, :, None] * bb_i[None] + pw_i[:t, :, :, None] * bb_r[None]
    kmat = (jnp.einsum('gdp,jgpc->jgdc', c_re, lb_r, precision=hp)
            - jnp.einsum('gdp,jgpc->jgdc', c_im, lb_i, precision=hp))

    q_r, q_i = pw_r[1:], pw_i[1:]
    wc_r = c_re[None] * q_r[:, :, None, :] - c_im[None] * q_i[:, :, None, :]
    wc_i = -(c_re[None] * q_i[:, :, None, :] + c_im[None] * q_r[:, :, None, :])
    w_carry = jnp.concatenate([wc_r, wc_i], axis=3)
    w_carry = w_carry.transpose(1, 3, 0, 2).reshape(S5_GROUPS, 2 * S5_STATE, S5_ROW)

    a1 = jnp.concatenate([pw_r[t], pw_r[t]], axis=1)
    a2 = jnp.concatenate([-pw_i[t], pw_i[t]], axis=1)

    gl = S5_GPB
    nblk = S5_GROUPS // gl
    ws = w_sum.astype(BF16).reshape(nblk, gl, t, S5_GROUP, 2 * S5_STATE).transpose(0, 2, 1, 3, 4)
    ws = ws.reshape(nblk, t * LANES, 2 * S5_STATE)

    def block_diag(tbl, rows_per_group, reps):
        col = jnp.arange(reps * LANES)
        src = jnp.arange(reps * S5_GROUP)
        spread = ((col[None, :] // LANES == src[:, None] // S5_GROUP)
                  & (col[None, :] % S5_GROUP == src[:, None] % S5_GROUP)).astype(BF16)
        wide = jnp.einsum('...rc,cl->...rl', tbl.astype(BF16), spread, preferred_element_type=F32)
        row_group = (jnp.arange(tbl.shape[-2]) // rows_per_group) % gl
        col_group = (col % LANES) // S5_GROUP
        return jnp.where(row_group[:, None] == col_group[None, :], wide, 0.0).astype(BF16)

    lag = block_diag(kmat.transpose(0, 1, 3, 2).reshape(t, S5_GROUPS * S5_GROUP, S5_GROUP), S5_GROUP, 1)
    lag = lag.reshape(t, nblk, LANES, LANES).transpose(1, 0, 2, 3)
    wc = block_diag(w_carry.reshape(S5_GROUPS * 2 * S5_STATE, S5_ROW), 2 * S5_STATE, t)
    wc = wc.reshape(nblk, gl * 2 * S5_STATE, t * LANES)
    return ws, lag, wc, a1, a2


def _s5_ssm(h, tables, batch, seq):
    w_sum, w_lag, w_carry, a1, a2 = tables
    nc = seq // S5_T
    s = _s5_sum(h, w_sum, batch, seq)
    s4 = s.reshape(batch, nc, S5_GROUPS, 2 * S5_STATE)
    xprev = _s5_scan(s4, a1, a2).reshape(s.shape)
    return _s5_out(h, xprev, w_lag, w_carry, batch, seq)


def kernel(x, norm_g, ffn_w_in, ffn_w_out, mix_w_in, attn_sinks, hgrn_lb, mix_w_out, s5_a_re, s5_a_im, s5_log_step, s5_b_re, s5_b_im, s5_c_re, s5_c_im, s5_d, s5_w_glu, final_g):
    batch, seq, d = x.shape
    n = batch * seq
    x = x.reshape(n, d)

    w_in16 = ffn_w_in.astype(BF16)
    w_out16 = ffn_w_out.astype(BF16)
    mix_in16 = mix_w_in.astype(BF16)
    mix_out16 = mix_w_out.astype(BF16)
    glu16 = s5_w_glu.astype(BF16)
    fg = final_g.reshape(1, d)

    lb_p = jax.nn.softmax(hgrn_lb.astype(F32), axis=0)
    lb_all = jnp.cumsum(lb_p, axis=0) - lb_p[0]

    for layer in range(DEPTH):
        g3 = norm_g[layer].reshape(3, 1, d)
        if layer % 2 == 0:
            e = layer // 2
            x = _ffn(x, g3[0], w_in16, w_out16, layer, 0, g3[1])
            proj = _norm_proj(x, g3[1], mix_in16[e])
            attn = _attention(proj, attn_sinks[e], batch, seq)
            rec = _hgrn(proj, lb_all[e], batch, seq)
            pre, pre_args = "mix", (attn, rec, mix_out16, e)
        else:
            o = layer // 2
            tables = _s5_tables(s5_a_re[o], s5_a_im[o], s5_log_step[o], s5_b_re[o], s5_b_im[o],
                                s5_c_re[o], s5_c_im[o])
            x, h = _ffn(x, g3[0], w_in16, w_out16, layer, 0, g3[1], post="also_norm")
            y = _s5_ssm(h, tables, batch, seq)
            pre, pre_args = "s5", (h, y, s5_d[o].reshape(1, d), glu16, o)
        last = layer == DEPTH - 1
        x = _ffn(x, g3[2], w_in16, w_out16, layer, 1, fg, pre=pre, pre_args=pre_args,
                 post="norm_only" if last else "none")
    return x.reshape(batch, seq, d)
```

```python
import functools
import math

import jax
import jax.numpy as jnp
import numpy as np
from jax import lax
from jax.experimental import pallas as pl
from jax.experimental.pallas import tpu as pltpu

F32 = jnp.float32
BF16 = jnp.bfloat16

EPS = 1e-6
DEPTH = 4
D_MODEL = 1024
D_FF = 2816
A_HEADS = 8
A_KV_HEADS = 2
A_GROUP = A_HEADS // A_KV_HEADS
A_HEAD_DIM = 64
WINDOW = 128
A_Q_W = A_HEADS * A_HEAD_DIM
A_KV_W = A_KV_HEADS * A_HEAD_DIM
B_HEADS = 4
B_DIM = 128
B_CHUNK = 32
B_W = B_HEADS * B_DIM
MIX_IN = A_Q_W + 2 * A_KV_W + 4 * B_W
S5_GROUP = 16
S5_GROUPS = D_MODEL // S5_GROUP
S5_STATE = 64
S5_T = 16
S5_ROW = S5_T * S5_GROUP

LANES = 128
VMEM_LIMIT = 56 * 1024 * 1024


def _cparams(*sem):
    return pltpu.CompilerParams(dimension_semantics=sem, vmem_limit_bytes=VMEM_LIMIT)


def _sigmoid(x):
    return 1.0 / (1.0 + jnp.exp(-x))


def _silu(x):
    return x * _sigmoid(x)


def _rms(x, g):
    return x * lax.rsqrt(jnp.mean(x * x, axis=-1, keepdims=True) + EPS) * g


def _dot(a, b):
    return jnp.dot(a, b, preferred_element_type=F32)


def _dot_nt(a, b):
    return lax.dot_general(a, b, (((1,), (1,)), ((), ())), preferred_element_type=F32)


def _mix_residual(x_ref, a_ref, b_ref, w_ref):
    ka = a_ref.shape[1]
    return x_ref[...] + _dot(a_ref[...], w_ref[:ka, :]) + _dot(b_ref[...], w_ref[ka:, :])


def _s5_residual(x_ref, h_ref, y_ref, d_ref, w_ref):
    y = y_ref[...] + d_ref[...] * h_ref[...]
    inner = math.sqrt(2.0 / math.pi) * (y + 0.044715 * (y * y * y))
    ge = y * (0.5 * (1.0 + jnp.tanh(inner)))
    z = _dot(ge.astype(BF16), w_ref[...])
    d = x_ref.shape[1]
    return x_ref[...] + z[:, :d] * _sigmoid(z[:, d:])


_FFN_PRE = {"none": (1, lambda x_ref: x_ref[...]), "mix": (4, _mix_residual), "s5": (5, _s5_residual)}


def _ffn_kernel(*refs, tf, pre, post):
    n_pre, pre_fn = _FFN_PRE[pre]
    g_ref, wi_ref, wo_ref, g2_ref = refs[n_pre:n_pre + 4]
    o_refs = refs[n_pre + 4:]
    x = pre_fn(*refs[:n_pre])
    h = _rms(x, g_ref[...]).astype(BF16)
    acc = None
    for c in range(D_FF // tf):
        gate = _dot(h, wi_ref[:, c * tf:(c + 1) * tf])
        up = _dot(h, wi_ref[:, D_FF + c * tf:D_FF + (c + 1) * tf])
        act = (_silu(gate) * up).astype(BF16)
        part = _dot(act, wo_ref[c * tf:(c + 1) * tf, :])
        acc = part if acc is None else acc + part
    y = x + 0.5 * acc
    if post == "norm_only":
        o_refs[0][...] = _rms(y, g2_ref[...])
    else:
        o_refs[0][...] = y
        if post == "also_norm":
            o_refs[1][...] = _rms(y, g2_ref[...])


def _resident(shape, *index):
    lead = len(index)
    block = (None,) * lead + tuple(shape)
    full = tuple(index) + (0,) * len(shape)
    return pl.BlockSpec(block, lambda *_: full, pipeline_mode=pl.Buffered(1))


def _row_tile(stream_bytes_per_row, temp_bytes_per_row, resident_bytes):
    for tm in (1024, 512, 256):
        if resident_bytes + tm * (2 * stream_bytes_per_row + temp_bytes_per_row) <= VMEM_LIMIT:
            return tm
    raise ValueError("row tile does not fit VMEM")


def _ffn(x, g, w_in, w_out, layer, which, g2, *, pre="none", pre_args=(), post="none", tf=256):
    n, d = x.shape
    two = post == "also_norm"
    row_ops = [x] + {"mix": list(pre_args[:2]), "s5": list(pre_args[:2]), "none": []}[pre]
    stream = sum(a.shape[1] * a.dtype.itemsize for a in row_ops) + (2 if two else 1) * d * 4
    resident = sum(math.prod(w.shape[-2:]) * w.dtype.itemsize
                   for w in [w_in, w_out] + ([pre_args[-2]] if pre != "none" else []))
    temp = d * (2 + 4 + 4) + 2 * tf * (4 + 4 + 2) + (2 * d * 4 if pre == "s5" else 0)
    tm = _row_tile(stream, temp, resident)

    rows = lambda a: pl.BlockSpec((tm, a.shape[1]), lambda i: (i, 0))
    tile = pl.BlockSpec((tm, d), lambda i: (i, 0))
    if pre == "mix":
        a, b, w, idx = pre_args
        pre_ops, pre_specs = (a, b, w), [rows(a), rows(b), _resident(w.shape[1:], idx)]
    elif pre == "s5":
        h, y, dskip, w, idx = pre_args
        pre_ops = (h, y, dskip, w)
        pre_specs = [tile, tile, _resident(dskip.shape), _resident(w.shape[1:], idx)]
    else:
        pre_ops, pre_specs = (), []
    shape = jax.ShapeDtypeStruct((n, d), F32)
    return pl.pallas_call(
        functools.partial(_ffn_kernel, tf=tf, pre=pre, post=post),
        grid=(n // tm,),
        in_specs=[tile] + pre_specs + [
            _resident((1, d)),
            _resident((d, 2 * D_FF), layer, which),
            _resident((D_FF, d), layer, which),
            _resident((1, d)),
        ],
        out_specs=[tile, tile] if two else tile,
        out_shape=[shape, shape] if two else shape,
        compiler_params=_cparams("parallel"),
        name="ffn",
    )(x, *pre_ops, g, w_in, w_out, g2)


def _norm_proj_kernel(x_ref, g_ref, w_ref, o_ref):
    h = _rms(x_ref[...], g_ref[...]).astype(BF16)
    o_ref[...] = _dot(h, w_ref[...])


def _norm_proj(x, g, w):
    n, d = x.shape
    m = w.shape[1]
    tm = _row_tile((d + m) * 4, d * 2, d * m * w.dtype.itemsize)
    return pl.pallas_call(
        _norm_proj_kernel,
        grid=(n // tm,),
        in_specs=[
            pl.BlockSpec((tm, d), lambda i: (i, 0)),
            _resident((1, d)),
            _resident((d, m)),
        ],
        out_specs=pl.BlockSpec((tm, m), lambda i: (i, 0)),
        out_shape=jax.ShapeDtypeStruct((n, m), F32),
        compiler_params=_cparams("parallel"),
        name="norm_proj",
    )(x, g, w)


def _attn_bias():
    w = WINDOW
    slopes = 2.0 ** (-8.0 * np.arange(1, A_HEADS + 1) / A_HEADS)
    dist = (np.arange(w) + w)[:, None] - np.arange(2 * w)[None, :]
    band = (dist >= 0) & (dist < w)
    has_prev = np.arange(2 * w)[None, :] >= w
    val = -slopes.reshape(A_KV_HEADS, A_GROUP, 1, 1) * dist.astype(np.float64)[None, None]
    out = np.stack([np.where(band & has_prev, val, -np.inf), np.where(band, val, -np.inf)])
    return jnp.asarray(out.reshape(2, A_KV_HEADS, A_GROUP * w, 2 * w), dtype=F32)


def _attn_kernel(sink_ref, bias_ref, q_ref, kc_ref, kp_ref, vc_ref, vp_ref, o_ref, *, nq):
    hk = pl.program_id(1)
    tile = pl.program_id(2)
    w = WINDOW
    dh = A_HEAD_DIM
    qw = A_GROUP * dh

    def widen(prev_ref, cur_ref):
        t = jnp.concatenate([prev_ref[...], cur_ref[...]], axis=0)
        r = pltpu.roll(t, dh, axis=1)
        lane = lax.broadcasted_iota(jnp.int32, t.shape, 1)
        t2 = jnp.where(lane // dh == hk, t, r)
        return jnp.concatenate([t2, t2], axis=1).astype(BF16)

    k4 = widen(kp_ref, kc_ref)
    v4 = widen(vp_ref, vc_ref)

    lane_head = lax.broadcasted_iota(jnp.int32, (w, qw), 1) // dh
    grp = lax.broadcasted_iota(jnp.int32, (A_GROUP * w, 1), 0) // w
    sink = jnp.zeros((A_GROUP * w, 1), F32)
    for g in range(A_GROUP):
        sink = jnp.where(grp == g, sink_ref[0, hk * A_GROUP + g], sink)

    bias_rest = bias_ref[1, 0]
    blocks = range(nq)
    keys = [slice(blk * w, (blk + 2) * w) for blk in blocks]

    scores = []
    for blk in blocks:
        q = q_ref[blk * w:(blk + 1) * w, :] * (dh ** -0.5)
        q4 = jnp.concatenate(
            [jnp.where(lane_head == g, q, 0.0) for g in range(A_GROUP)], axis=0).astype(BF16)
        bias = jnp.where(tile == 0, bias_ref[0, 0], bias_rest) if blk == 0 else bias_rest
        scores.append(_dot_nt(q4, k4[keys[blk]]) + bias)

    probs, denoms = [], []
    for s in scores:
        m = jnp.maximum(jnp.max(s, axis=-1, keepdims=True), sink)
        p = jnp.exp(s - m)
        denoms.append(jnp.sum(p, axis=-1, keepdims=True) + jnp.exp(sink - m))
        probs.append(p.astype(BF16))

    outs = [_dot(probs[blk], v4[keys[blk]]) / denoms[blk] for blk in blocks]

    for blk in blocks:
        o = jnp.zeros((w, qw), F32)
        for g in range(A_GROUP):
            o = o + jnp.where(lane_head == g, outs[blk][g * w:(g + 1) * w, :], 0.0)
        o_ref[blk * w:(blk + 1) * w, :] = o.astype(o_ref.dtype)


def _attention(proj, sinks, batch, seq, *, nq=8):
    n = proj.shape[0]
    rows = nq * WINDOW
    nt = seq // rows
    qw = A_GROUP * A_HEAD_DIM
    kcol = A_Q_W // LANES
    vcol = (A_Q_W + A_KV_W) // LANES
    cur = lambda b, h, i: b * nt + i
    prev = lambda b, h, i: b * nt * nq + jnp.maximum(i * nq - 1, 0)
    bias = _attn_bias()
    return pl.pallas_call(
        functools.partial(_attn_kernel, nq=nq),
        grid=(batch, A_KV_HEADS, nt),
        in_specs=[
            pl.BlockSpec(memory_space=pltpu.SMEM),
            pl.BlockSpec((2, 1) + bias.shape[2:], lambda b, h, i: (0, h, 0, 0)),
            pl.BlockSpec((rows, qw), lambda b, h, i: (cur(b, h, i), h)),
            pl.BlockSpec((rows, LANES), lambda b, h, i: (cur(b, h, i), kcol)),
            pl.BlockSpec((WINDOW, LANES), lambda b, h, i: (prev(b, h, i), kcol)),
            pl.BlockSpec((rows, LANES), lambda b, h, i: (cur(b, h, i), vcol)),
            pl.BlockSpec((WINDOW, LANES), lambda b, h, i: (prev(b, h, i), vcol)),
        ],
        out_specs=pl.BlockSpec((rows, qw), lambda b, h, i: (cur(b, h, i), h)),
        out_shape=jax.ShapeDtypeStruct((n, A_Q_W), BF16),
        compiler_params=_cparams("parallel", "parallel", "parallel"),
        name="swa",
    )(sinks.reshape(1, A_HEADS), bias, proj, proj, proj, proj, proj)


def _split3(x):
    hi = x.astype(BF16)
    r1 = x - hi.astype(F32)
    mid = r1.astype(BF16)
    lo = (r1 - mid.astype(F32)).astype(BF16)
    return hi, mid, lo


def _hgrn_kernel(lb_ref, q_ref, f_ref, i_ref, g_ref, o_ref, st_ref, *, ts, heads):
    @pl.when(pl.program_id(2) == 0)
    def _():
        st_ref[...] = jnp.zeros_like(st_ref)

    lanes = [slice(hd * B_DIM, (hd + 1) * B_DIM) for hd in range(heads)]
    args = [(lb_ref[:, ln], q_ref[:, ln], f_ref[:, ln], i_ref[:, ln], g_ref[:, ln], st_ref[hd])
            for hd, ln in enumerate(lanes)]
    results = [_hgrn_head(*a, ts) for a in args]
    for hd, (out, st) in enumerate(results):
        st_ref[hd] = st
        o_ref[:, lanes[hd]] = out.astype(o_ref.dtype)


def _hgrn_head(lb, q, f_logit, v32, g, st, ts):
    c = B_CHUNK
    nchunk = ts // c
    f = lb + (1.0 - lb) * _sigmoid(f_logit)
    k = 1.0 - f
    logf = jnp.log(f)

    sub = LANES
    r2 = lax.broadcasted_iota(jnp.int32, (sub, sub), 0)
    c2 = lax.broadcasted_iota(jnp.int32, (sub, sub), 1)
    causal = ((r2 // c) == (c2 // c)) & (c2 <= r2)

    tri = causal.astype(BF16)
    parts = _split3(logf)
    cum = jnp.concatenate(
        [sum(_dot(tri, p[s0:s0 + sub]) for p in parts) for s0 in range(0, ts, sub)], axis=0)
    tot_rows = [cum[(ch + 1) * c - 1:(ch + 1) * c, :] for ch in range(nchunk)]
    tot = jnp.concatenate([jnp.broadcast_to(t, (c, B_DIM)) for t in tot_rows], axis=0)
    dec_rows = [jnp.exp(t) for t in tot_rows]

    qd = _silu(q) * jnp.exp(cum)
    kinv = (k * jnp.exp(-cum)).astype(BF16)
    kend = k * jnp.exp(tot - cum)
    v16 = v32.astype(BF16)
    v_t = v32.T.astype(BF16)

    per = sub // c
    chunk_gap = r2 // c - c2 // c
    one = jnp.ones((1, B_DIM), F32)

    def by_chunk(rows):
        return jnp.concatenate([jnp.broadcast_to(x, (c, B_DIM)) for x in rows], axis=0)

    groups = range(ts // sub)
    rows = [slice(s * sub, (s + 1) * sub) for s in groups]
    decs = [dec_rows[s * per:(s + 1) * per] for s in groups]

    states = []
    for s in groups:
        d = decs[s]
        states.append(st.astype(BF16))
        to_end = by_chunk([d[1] * d[2] * d[3], d[2] * d[3], d[3], one])
        kv_t = _dot(v_t[:, rows[s]], (kend[rows[s]] * to_end).astype(BF16))
        st = st * (d[0] * d[1] * d[2] * d[3]) + kv_t

    scores = []
    for s in groups:
        d = decs[s]
        between1 = by_chunk([one, one, d[1], d[2]])
        between2 = by_chunk([one, one, one, d[1] * d[2]])
        qs = qd[rows[s]]
        same = _dot_nt(qs.astype(BF16), kinv[rows[s]])
        lhs = jnp.concatenate([qs, qs * between1, qs * between2], axis=0).astype(BF16)
        cross = _dot_nt(lhs, kend[rows[s]].astype(BF16))
        sc = jnp.where(causal, same,
                       jnp.where(chunk_gap == 1, cross[:sub],
                                 jnp.where(chunk_gap == 2, cross[sub:2 * sub],
                                           jnp.where(chunk_gap == 3, cross[2 * sub:], 0.0))))
        scores.append(sc.astype(BF16))

    outs = []
    for s in groups:
        d = decs[s]
        from_start = by_chunk([one, d[0], d[0] * d[1], d[0] * d[1] * d[2]])
        outs.append(_dot(scores[s], v16[rows[s]])
                    + _dot_nt((qd[rows[s]] * from_start).astype(BF16), states[s]))
    o = jnp.concatenate(outs, axis=0)

    o = o * lax.rsqrt(jnp.mean(o * o, axis=-1, keepdims=True) + EPS)
    return o * _silu(g), st


def _hgrn(proj, lb, batch, seq, *, ts=2048, heads=1):
    n = proj.shape[0]
    nt = seq // ts
    w = heads * B_DIM
    assert (A_Q_W + 2 * A_KV_W) % w == 0 and B_W % w == 0
    base = (A_Q_W + 2 * A_KV_W) // w
    ng = B_W // w

    def col(kind):
        return lambda b, h, t: (b * nt + t, base + kind * ng + h)

    return pl.pallas_call(
        functools.partial(_hgrn_kernel, ts=ts, heads=heads),
        grid=(batch, ng, nt),
        in_specs=[
            pl.BlockSpec((1, w), lambda b, h, t: (0, h)),
            pl.BlockSpec((ts, w), col(0)),
            pl.BlockSpec((ts, w), col(1)),
            pl.BlockSpec((ts, w), col(2)),
            pl.BlockSpec((ts, w), col(3)),
        ],
        out_specs=pl.BlockSpec((ts, w), lambda b, h, t: (b * nt + t, h)),
        out_shape=jax.ShapeDtypeStruct((n, B_W), BF16),
        scratch_shapes=[pltpu.VMEM((heads, B_DIM, B_DIM), F32)],
        compiler_params=_cparams("parallel", "parallel", "arbitrary"),
        name="hgrn2",
    )(lb.reshape(1, B_W), proj, proj, proj, proj)


def _s5_rows(h_ref, nc):
    return jnp.concatenate(
        [h_ref[pl.ds(t, nc, stride=S5_T), :] for t in range(S5_T)], axis=1).astype(BF16)


S5_GPB = LANES // S5_GROUP


def _s5_sum_kernel(h_ref, w_ref, o_ref, wx_ref, *, nc):
    k = S5_T * LANES

    @pl.when(pl.program_id(1) == 0)
    def _():
        w = w_ref[0]
        row_group = (lax.broadcasted_iota(jnp.int32, (k, LANES), 0) % LANES) // S5_GROUP
        for g in range(S5_GPB):
            wx_ref[:, g * LANES:(g + 1) * LANES] = jnp.where(row_group == g, w, jnp.zeros_like(w))

    s = _dot(_s5_rows(h_ref, nc), wx_ref[...])
    for g in range(S5_GPB):
        o_ref[:, g, :] = s[:, g * LANES:(g + 1) * LANES]


def _s5_sum(h, w, batch, seq):
    nc = seq // S5_T
    nblk, k, _ = w.shape
    return pl.pallas_call(
        functools.partial(_s5_sum_kernel, nc=nc),
        grid=(nblk, batch),
        in_specs=[pl.BlockSpec((seq, LANES), lambda j, b: (b, j)),
                  pl.BlockSpec((1, k, LANES), lambda j, b: (j, 0, 0))],
        out_specs=pl.BlockSpec((nc, S5_GPB, LANES), lambda j, b: (b, j, 0)),
        out_shape=jax.ShapeDtypeStruct((batch * nc, S5_GROUPS, LANES), F32),
        scratch_shapes=[pltpu.VMEM((k, S5_GPB * LANES), BF16)],
        compiler_params=_cparams("parallel", "arbitrary"),
        name="s5_chunk_sum",
    )(h, w)


def _s5_scan_kernel(s_ref, a1_ref, a2_ref, o_ref, x_ref, ss_ref, *, tc, nb):
    @pl.when(pl.program_id(0) == 0)
    def _():
        x_ref[...] = jnp.zeros_like(x_ref)

    a1 = a1_ref[...]
    a2 = a2_ref[...]

    rows, lanes = a1.shape
    for b in range(nb):
        blk = s_ref[b].reshape(tc * rows, lanes)
        ss_ref[b] = pltpu.roll(blk, S5_STATE, axis=1).reshape(tc, rows, lanes)

    def run(batches):
        def step(c, carry):
            new = []
            for b, (x, xs) in zip(batches, carry):
                o_ref[b, c] = x
                new.append((a1 * x + a2 * xs + s_ref[b, c], a1 * xs - a2 * x + ss_ref[b, c]))
            return tuple(new)

        init = tuple((x_ref[b], pltpu.roll(x_ref[b], S5_STATE, axis=1)) for b in batches)
        final = lax.fori_loop(0, tc, step, init, unroll=4)
        for b, (x, _) in zip(batches, final):
            x_ref[b] = x

    for b0 in range(0, nb, 2):
        run(range(b0, min(b0 + 2, nb)))


def _s5_scan(s, a1, a2, *, tc=32):
    nb, nc, rows, lanes = s.shape
    return pl.pallas_call(
        functools.partial(_s5_scan_kernel, tc=tc, nb=nb),
        grid=(nc // tc,),
        in_specs=[pl.BlockSpec((nb, tc, rows, lanes), lambda i: (0, i, 0, 0)),
                  pl.BlockSpec((rows, lanes), lambda i: (0, 0)),
                  pl.BlockSpec((rows, lanes), lambda i: (0, 0))],
        out_specs=pl.BlockSpec((nb, tc, rows, lanes), lambda i: (0, i, 0, 0)),
        out_shape=jax.ShapeDtypeStruct(s.shape, F32),
        scratch_shapes=[pltpu.VMEM((nb, rows, lanes), F32), pltpu.VMEM((nb, tc, rows, lanes), F32)],
        compiler_params=_cparams("arbitrary"),
        name="s5_chunk_scan",
    )(s, a1, a2)


def _s5_out_kernel(h_ref, x_ref, lag_ref, wc_ref, o_ref, wt_ref, *, nc):
    t = S5_T
    half_t = t // 2
    half = half_t * LANES

    @pl.when(pl.program_id(1) == 0)
    def _():
        zero = jnp.zeros((LANES, LANES), BF16)
        for s in range(t):
            for tp in range(half_t if s >= half_t else 0, t):
                blk = lag_ref[0, tp - s] if tp >= s else zero
                wt_ref[s * LANES:(s + 1) * LANES, tp * LANES:(tp + 1) * LANES] = blk

    a = _s5_rows(h_ref, nc)
    xp = jnp.concatenate([x_ref[:, g, :] for g in range(S5_GPB)], axis=1).astype(BF16)
    y0 = _dot(a[:, :half], wt_ref[:half, :half]) + _dot(xp, wc_ref[0, :, :half])
    y1 = _dot(a, wt_ref[:, half:]) + _dot(xp, wc_ref[0, :, half:])
    for tt in range(half_t):
        o_ref[pl.ds(tt, nc, stride=t), :] = y0[:, tt * LANES:(tt + 1) * LANES]
        o_ref[pl.ds(half_t + tt, nc, stride=t), :] = y1[:, tt * LANES:(tt + 1) * LANES]


def _s5_out(h, xprev, w_lag, w_carry, batch, seq):
    nc = seq // S5_T
    nblk = w_lag.shape[0]
    kc = w_carry.shape[1]
    k = S5_T * LANES
    return pl.pallas_call(
        functools.partial(_s5_out_kernel, nc=nc),
        grid=(nblk, batch),
        in_specs=[pl.BlockSpec((seq, LANES), lambda j, b: (b, j)),
                  pl.BlockSpec((nc, S5_GPB, LANES), lambda j, b: (b, j, 0)),
                  pl.BlockSpec((1,) + w_lag.shape[1:], lambda j, b: (j, 0, 0, 0)),
                  pl.BlockSpec((1, kc, k), lambda j, b: (j, 0, 0))],
        out_specs=pl.BlockSpec((seq, LANES), lambda j, b: (b, j)),
        out_shape=jax.ShapeDtypeStruct(h.shape, F32),
        scratch_shapes=[pltpu.VMEM((k, k), BF16)],
        compiler_params=_cparams("parallel", "arbitrary"),
        name="s5_chunk_out",
    )(h, xprev, w_lag, w_carry)


def _s5_tables(a_re, a_im, log_step, b_re, b_im, c_re, c_im):
    hp = lax.Precision.HIGHEST
    t = S5_T
    step = jnp.exp(log_step)[:, None]
    mag = jnp.exp(step * a_re)
    ab_r = mag * jnp.cos(step * a_im)
    ab_i = mag * jnp.sin(step * a_im)
    den = a_re * a_re + a_im * a_im
    coef_r = ((ab_r - 1.0) * a_re + ab_i * a_im) / den
    coef_i = (ab_i * a_re - (ab_r - 1.0) * a_im) / den
    bb_r = coef_r[..., None] * b_re - coef_i[..., None] * b_im
    bb_i = coef_r[..., None] * b_im + coef_i[..., None] * b_re
    j = jnp.arange(t + 1, dtype=F32)[:, None, None]
    pmag = jnp.exp(j * (step * a_re)[None])
    pw_r = pmag * jnp.cos(j * (step * a_im)[None])
    pw_i = pmag * jnp.sin(j * (step * a_im)[None])

    rev_r, rev_i = pw_r[t - 1::-1], pw_i[t - 1::-1]
    ws_r = rev_r[:, :, :, None] * bb_r[None] - rev_i[:, :, :, None] * bb_i[None]
    ws_i = rev_r[:, :, :, None] * bb_i[None] + rev_i[:, :, :, None] * bb_r[None]
    w_sum = jnp.concatenate([ws_r, ws_i], axis=2)
    w_sum = w_sum.transpose(1, 0, 3, 2).reshape(S5_GROUPS, S5_ROW, 2 * S5_STATE)

    lb_r = pw_r[:t, :, :, None] * bb_r[None] - pw_i[:t, :, :, None] * bb_i[None]
    lb_i = pw_r[:t, :, :, None] * bb_i[None] + pw_i[:t, :, :, None] * bb_r[None]
    kmat = (jnp.einsum('gdp,jgpc->jgdc', c_re, lb_r, precision=hp)
            - jnp.einsum('gdp,jgpc->jgdc', c_im, lb_i, precision=hp))

    q_r, q_i = pw_r[1:], pw_i[1:]
    wc_r = c_re[None] * q_r[:, :, None, :] - c_im[None] * q_i[:, :, None, :]
    wc_i = -(c_re[None] * q_i[:, :, None, :] + c_im[None] * q_r[:, :, None, :])
    w_carry = jnp.concatenate([wc_r, wc_i], axis=3)
    w_carry = w_carry.transpose(1, 3, 0, 2).reshape(S5_GROUPS, 2 * S5_STATE, S5_ROW)

    a1 = jnp.concatenate([pw_r[t], pw_r[t]], axis=1)
    a2 = jnp.concatenate([-pw_i[t], pw_i[t]], axis=1)

    gl = S5_GPB
    nblk = S5_GROUPS // gl
    ws = w_sum.astype(BF16).reshape(nblk, gl, t, S5_GROUP, 2 * S5_STATE).transpose(0, 2, 1, 3, 4)
    ws = ws.reshape(nblk, t * LANES, 2 * S5_STATE)

    def block_diag(tbl, rows_per_group, reps):
        col = jnp.arange(reps * LANES)
        src = jnp.arange(reps * S5_GROUP)
        spread = ((col[None, :] // LANES == src[:, None] // S5_GROUP)
                  & (col[None, :] % S5_GROUP == src[:, None] % S5_GROUP)).astype(BF16)
        wide = jnp.einsum('...rc,cl->...rl', tbl.astype(BF16), spread, preferred_element_type=F32)
        row_group = (jnp.arange(tbl.shape[-2]) // rows_per_group) % gl
        col_group = (col % LANES) // S5_GROUP
        return jnp.where(row_group[:, None] == col_group[None, :], wide, 0.0).astype(BF16)

    lag = block_diag(kmat.transpose(0, 1, 3, 2).reshape(t, S5_GROUPS * S5_GROUP, S5_GROUP), S5_GROUP, 1)
    lag = lag.reshape(t, nblk, LANES, LANES).transpose(1, 0, 2, 3)
    wc = block_diag(w_carry.reshape(S5_GROUPS * 2 * S5_STATE, S5_ROW), 2 * S5_STATE, t)
    wc = wc.reshape(nblk, gl * 2 * S5_STATE, t * LANES)
    return ws, lag, wc, a1, a2


def _s5_ssm(h, tables, batch, seq):
    w_sum, w_lag, w_carry, a1, a2 = tables
    nc = seq // S5_T
    s = _s5_sum(h, w_sum, batch, seq)
    s4 = s.reshape(batch, nc, S5_GROUPS, 2 * S5_STATE)
    xprev = _s5_scan(s4, a1, a2).reshape(s.shape)
    return _s5_out(h, xprev, w_lag, w_carry, batch, seq)


def kernel(x, norm_g, ffn_w_in, ffn_w_out, mix_w_in, attn_sinks, hgrn_lb, mix_w_out, s5_a_re, s5_a_im, s5_log_step, s5_b_re, s5_b_im, s5_c_re, s5_c_im, s5_d, s5_w_glu, final_g):
    batch, seq, d = x.shape
    n = batch * seq
    x = x.reshape(n, d)

    w_in16 = ffn_w_in.astype(BF16)
    w_out16 = ffn_w_out.astype(BF16)
    mix_in16 = mix_w_in.astype(BF16)
    mix_out16 = mix_w_out.astype(BF16)
    glu16 = s5_w_glu.astype(BF16)
    fg = final_g.reshape(1, d)

    lb_p = jax.nn.softmax(hgrn_lb.astype(F32), axis=0)
    lb_all = jnp.cumsum(lb_p, axis=0) - lb_p[0]

    for layer in range(DEPTH):
        g3 = norm_g[layer].reshape(3, 1, d)
        if layer % 2 == 0:
            e = layer // 2
            x = _ffn(x, g3[0], w_in16, w_out16, layer, 0, g3[1])
            proj = _norm_proj(x, g3[1], mix_in16[e])
            attn = _attention(proj, attn_sinks[e], batch, seq)
            rec = _hgrn(proj, lb_all[e], batch, seq)
            pre, pre_args = "mix", (attn, rec, mix_out16, e)
        else:
            o = layer // 2
            tables = _s5_tables(s5_a_re[o], s5_a_im[o], s5_log_step[o], s5_b_re[o], s5_b_im[o],
                                s5_c_re[o], s5_c_im[o])
            x, h = _ffn(x, g3[0], w_in16, w_out16, layer, 0, g3[1], post="also_norm")
            y = _s5_ssm(h, tables, batch, seq)
            pre, pre_args = "s5", (h, y, s5_d[o].reshape(1, d), glu16, o)
        last = layer == DEPTH - 1
        x = _ffn(x, g3[2], w_in16, w_out16, layer, 1, fg, pre=pre, pre_args=pre_args,
                 post="norm_only" if last else "none")
    return x.reshape(batch, seq, d)
```

```python
import functools
import math

import jax
import jax.numpy as jnp
import numpy as np
from jax import lax
from jax.experimental import pallas as pl
from jax.experimental.pallas import tpu as pltpu

F32 = jnp.float32
BF16 = jnp.bfloat16

EPS = 1e-6
DEPTH = 4
D_MODEL = 1024
D_FF = 2816
A_HEADS = 8
A_KV_HEADS = 2
A_GROUP = A_HEADS // A_KV_HEADS
A_HEAD_DIM = 64
WINDOW = 128
A_Q_W = A_HEADS * A_HEAD_DIM
A_KV_W = A_KV_HEADS * A_HEAD_DIM
B_HEADS = 4
B_DIM = 128
B_CHUNK = 32
B_W = B_HEADS * B_DIM
MIX_IN = A_Q_W + 2 * A_KV_W + 4 * B_W
S5_GROUP = 16
S5_GROUPS = D_MODEL // S5_GROUP
S5_STATE = 64
S5_T = 16
S5_ROW = S5_T * S5_GROUP

LANES = 128
VMEM_LIMIT = 56 * 1024 * 1024


def _cparams(*sem):
    return pltpu.CompilerParams(dimension_semantics=sem, vmem_limit_bytes=VMEM_LIMIT)


def _sigmoid(x):
    return 1.0 / (1.0 + jnp.exp(-x))


def _silu(x):
    return x * _sigmoid(x)


def _rms(x, g):
    return x * lax.rsqrt(jnp.mean(x * x, axis=-1, keepdims=True) + EPS) * g


def _dot(a, b):
    return jnp.dot(a, b, preferred_element_type=F32)


def _dot_nt(a, b):
    return lax.dot_general(a, b, (((1,), (1,)), ((), ())), preferred_element_type=F32)


def _mix_residual(x_ref, a_ref, b_ref, w_ref):
    ka = a_ref.shape[1]
    return x_ref[...] + _dot(a_ref[...], w_ref[:ka, :]) + _dot(b_ref[...], w_ref[ka:, :])


def _s5_residual(x_ref, h_ref, y_ref, d_ref, w_ref):
    y = y_ref[...] + d_ref[...] * h_ref[...]
    inner = math.sqrt(2.0 / math.pi) * (y + 0.044715 * (y * y * y))
    ge = y * (0.5 * (1.0 + jnp.tanh(inner)))
    z = _dot(ge.astype(BF16), w_ref[...])
    d = x_ref.shape[1]
    return x_ref[...] + z[:, :d] * _sigmoid(z[:, d:])


_FFN_PRE = {"none": (1, lambda x_ref: x_ref[...]), "mix": (4, _mix_residual), "s5": (5, _s5_residual)}


def _ffn_kernel(*refs, tf, pre, post):
    n_pre, pre_fn = _FFN_PRE[pre]
    g_ref, wi_ref, wo_ref, g2_ref = refs[n_pre:n_pre + 4]
    o_refs = refs[n_pre + 4:]
    x = pre_fn(*refs[:n_pre])
    h = _rms(x, g_ref[...]).astype(BF16)
    acc = None
    for c in range(D_FF // tf):
        gate = _dot(h, wi_ref[:, c * tf:(c + 1) * tf])
        up = _dot(h, wi_ref[:, D_FF + c * tf:D_FF + (c + 1) * tf])
        act = (_silu(gate) * up).astype(BF16)
        part = _dot(act, wo_ref[c * tf:(c + 1) * tf, :])
        acc = part if acc is None else acc + part
    y = x + 0.5 * acc
    if post == "norm_only":
        o_refs[0][...] = _rms(y, g2_ref[...])
    else:
        o_refs[0][...] = y
        if post == "also_norm":
            o_refs[1][...] = _rms(y, g2_ref[...])


def _resident(shape, *index):
    lead = len(index)
    block = (None,) * lead + tuple(shape)
    full = tuple(index) + (0,) * len(shape)
    return pl.BlockSpec(block, lambda *_: full, pipeline_mode=pl.Buffered(1))


def _row_tile(stream_bytes_per_row, temp_bytes_per_row, resident_bytes):
    for tm in (1024, 512, 256):
        if resident_bytes + tm * (2 * stream_bytes_per_row + temp_bytes_per_row) <= VMEM_LIMIT:
            return tm
    raise ValueError("row tile does not fit VMEM")


def _ffn(x, g, w_in, w_out, layer, which, g2, *, pre="none", pre_args=(), post="none", tf=256):
    n, d = x.shape
    two = post == "also_norm"
    row_ops = [x] + {"mix": list(pre_args[:2]), "s5": list(pre_args[:2]), "none": []}[pre]
    stream = sum(a.shape[1] * a.dtype.itemsize for a in row_ops) + (2 if two else 1) * d * 4
    resident = sum(math.prod(w.shape[-2:]) * w.dtype.itemsize
                   for w in [w_in, w_out] + ([pre_args[-2]] if pre != "none" else []))
    temp = d * (2 + 4 + 4) + 2 * tf * (4 + 4 + 2) + (2 * d * 4 if pre == "s5" else 0)
    tm = _row_tile(stream, temp, resident)

    rows = lambda a: pl.BlockSpec((tm, a.shape[1]), lambda i: (i, 0))
    tile = pl.BlockSpec((tm, d), lambda i: (i, 0))
    if pre == "mix":
        a, b, w, idx = pre_args
        pre_ops, pre_specs = (a, b, w), [rows(a), rows(b), _resident(w.shape[1:], idx)]
    elif pre == "s5":
        h, y, dskip, w, idx = pre_args
        pre_ops = (h, y, dskip, w)
        pre_specs = [tile, tile, _resident(dskip.shape), _resident(w.shape[1:], idx)]
    else:
        pre_ops, pre_specs = (), []
    shape = jax.ShapeDtypeStruct((n, d), F32)
    return pl.pallas_call(
        functools.partial(_ffn_kernel, tf=tf, pre=pre, post=post),
        grid=(n // tm,),
        in_specs=[tile] + pre_specs + [
            _resident((1, d)),
            _resident((d, 2 * D_FF), layer, which),
            _resident((D_FF, d), layer, which),
            _resident((1, d)),
        ],
        out_specs=[tile, tile] if two else tile,
        out_shape=[shape, shape] if two else shape,
        compiler_params=_cparams("parallel"),
        name="ffn",
    )(x, *pre_ops, g, w_in, w_out, g2)


def _norm_proj_kernel(x_ref, g_ref, w_ref, o_ref):
    h = _rms(x_ref[...], g_ref[...]).astype(BF16)
    res = _dot(h, w_ref[...])
    for cb in range(o_ref.shape[0]):
        o_ref[cb] = res[:, cb * LANES:(cb + 1) * LANES]


def _norm_proj(x, g, w):
    n, d = x.shape
    m = w.shape[1]
    tm = _row_tile((d + m) * 4, d * 2 + m * 4, d * m * w.dtype.itemsize)
    return pl.pallas_call(
        _norm_proj_kernel,
        grid=(n // tm,),
        in_specs=[
            pl.BlockSpec((tm, d), lambda i: (i, 0)),
            _resident((1, d)),
            _resident((d, m)),
        ],
        out_specs=pl.BlockSpec((m // LANES, tm, LANES), lambda i: (0, i, 0)),
        out_shape=jax.ShapeDtypeStruct((m // LANES, n, LANES), F32),
        compiler_params=_cparams("parallel"),
        name="norm_proj",
    )(x, g, w)


def _attn_bias():
    w = WINDOW
    slopes = 2.0 ** (-8.0 * np.arange(1, A_HEADS + 1) / A_HEADS)
    dist = (np.arange(w) + w)[:, None] - np.arange(2 * w)[None, :]
    band = (dist >= 0) & (dist < w)
    has_prev = np.arange(2 * w)[None, :] >= w
    val = -slopes.reshape(A_KV_HEADS, A_GROUP, 1, 1) * dist.astype(np.float64)[None, None]
    out = np.stack([np.where(band & has_prev, val, -np.inf), np.where(band, val, -np.inf)])
    return jnp.asarray(out.reshape(2, A_KV_HEADS, A_GROUP * w, 2 * w), dtype=F32)


def _attn_kernel(sink_ref, bias_ref, q_ref, kc_ref, kp_ref, vc_ref, vp_ref, o_ref, *, nq):
    hk = pl.program_id(1)
    tile = pl.program_id(2)
    w = WINDOW
    dh = A_HEAD_DIM
    qw = A_GROUP * dh

    def widen(prev_ref, cur_ref):
        t = jnp.concatenate([prev_ref[...], cur_ref[...]], axis=0)
        r = pltpu.roll(t, dh, axis=1)
        lane = lax.broadcasted_iota(jnp.int32, t.shape, 1)
        t2 = jnp.where(lane // dh == hk, t, r)
        return jnp.concatenate([t2, t2], axis=1).astype(BF16)

    k4 = widen(kp_ref, kc_ref)
    v4 = widen(vp_ref, vc_ref)

    lane_head = lax.broadcasted_iota(jnp.int32, (w, qw), 1) // dh
    grp = lax.broadcasted_iota(jnp.int32, (A_GROUP * w, 1), 0) // w
    sink = jnp.zeros((A_GROUP * w, 1), F32)
    for g in range(A_GROUP):
        sink = jnp.where(grp == g, sink_ref[0, hk * A_GROUP + g], sink)

    bias_rest = bias_ref[1, 0]
    blocks = range(nq)
    keys = [slice(blk * w, (blk + 2) * w) for blk in blocks]

    scores = []
    for blk in blocks:
        q = jnp.concatenate([q_ref[c, blk * w:(blk + 1) * w, :] for c in range(qw // LANES)], axis=1)
        q = q * (dh ** -0.5)
        q4 = jnp.concatenate(
            [jnp.where(lane_head == g, q, 0.0) for g in range(A_GROUP)], axis=0).astype(BF16)
        bias = jnp.where(tile == 0, bias_ref[0, 0], bias_rest) if blk == 0 else bias_rest
        scores.append(_dot_nt(q4, k4[keys[blk]]) + bias)

    probs, denoms = [], []
    for s in scores:
        m = jnp.maximum(jnp.max(s, axis=-1, keepdims=True), sink)
        p = jnp.exp(s - m)
        denoms.append(jnp.sum(p, axis=-1, keepdims=True) + jnp.exp(sink - m))
        probs.append(p.astype(BF16))

    outs = [_dot(probs[blk], v4[keys[blk]]) / denoms[blk] for blk in blocks]

    for blk in blocks:
        o = jnp.zeros((w, qw), F32)
        for g in range(A_GROUP):
            o = o + jnp.where(lane_head == g, outs[blk][g * w:(g + 1) * w, :], 0.0)
        o_ref[blk * w:(blk + 1) * w, :] = o.astype(o_ref.dtype)


def _attention(proj, sinks, batch, seq, *, nq=8):
    n = proj.shape[1]
    rows = nq * WINDOW
    nt = seq // rows
    qw = A_GROUP * A_HEAD_DIM
    qblk = qw // LANES
    kcol = A_Q_W // LANES
    vcol = (A_Q_W + A_KV_W) // LANES
    cur = lambda b, h, i: b * nt + i
    prev = lambda b, h, i: b * nt * nq + jnp.maximum(i * nq - 1, 0)
    bias = _attn_bias()
    return pl.pallas_call(
        functools.partial(_attn_kernel, nq=nq),
        grid=(batch, A_KV_HEADS, nt),
        in_specs=[
            pl.BlockSpec(memory_space=pltpu.SMEM),
            pl.BlockSpec((2, 1) + bias.shape[2:], lambda b, h, i: (0, h, 0, 0)),
            pl.BlockSpec((qblk, rows, LANES), lambda b, h, i: (h, cur(b, h, i), 0)),
            pl.BlockSpec((None, rows, LANES), lambda b, h, i: (kcol, cur(b, h, i), 0)),
            pl.BlockSpec((None, WINDOW, LANES), lambda b, h, i: (kcol, prev(b, h, i), 0)),
            pl.BlockSpec((None, rows, LANES), lambda b, h, i: (vcol, cur(b, h, i), 0)),
            pl.BlockSpec((None, WINDOW, LANES), lambda b, h, i: (vcol, prev(b, h, i), 0)),
        ],
        out_specs=pl.BlockSpec((rows, qw), lambda b, h, i: (cur(b, h, i), h)),
        out_shape=jax.ShapeDtypeStruct((n, A_Q_W), BF16),
        compiler_params=_cparams("parallel", "parallel", "parallel"),
        name="swa",
    )(sinks.reshape(1, A_HEADS), bias, proj, proj, proj, proj, proj)


def _split3(x):
    hi = x.astype(BF16)
    r1 = x - hi.astype(F32)
    mid = r1.astype(BF16)
    lo = (r1 - mid.astype(F32)).astype(BF16)
    return hi, mid, lo


def _hgrn_kernel(lb_ref, q_ref, f_ref, i_ref, g_ref, o_ref, st_ref, *, ts):
    @pl.when(pl.program_id(2) == 0)
    def _():
        st_ref[...] = jnp.zeros_like(st_ref)

    out, st = _hgrn_head(lb_ref[...], q_ref[...], f_ref[...], i_ref[...], g_ref[...], st_ref[...], ts)
    st_ref[...] = st
    o_ref[...] = out.astype(o_ref.dtype)


def _hgrn_head(lb, q, f_logit, v32, g, st, ts):
    c = B_CHUNK
    nchunk = ts // c
    f = lb + (1.0 - lb) * _sigmoid(f_logit)
    k = 1.0 - f
    logf = jnp.log(f)

    sub = LANES
    r2 = lax.broadcasted_iota(jnp.int32, (sub, sub), 0)
    c2 = lax.broadcasted_iota(jnp.int32, (sub, sub), 1)
    causal = ((r2 // c) == (c2 // c)) & (c2 <= r2)

    tri = causal.astype(BF16)
    parts = _split3(logf)
    cum = jnp.concatenate(
        [sum(_dot(tri, p[s0:s0 + sub]) for p in parts) for s0 in range(0, ts, sub)], axis=0)
    tot_rows = [cum[(ch + 1) * c - 1:(ch + 1) * c, :] for ch in range(nchunk)]
    tot = jnp.concatenate([jnp.broadcast_to(t, (c, B_DIM)) for t in tot_rows], axis=0)
    dec_rows = [jnp.exp(t) for t in tot_rows]

    qd = _silu(q) * jnp.exp(cum)
    kinv = (k * jnp.exp(-cum)).astype(BF16)
    kend = k * jnp.exp(tot - cum)
    v16 = v32.astype(BF16)
    v_t = v32.T.astype(BF16)

    per = sub // c
    chunk_gap = r2 // c - c2 // c
    gap1, gap2, gap3 = chunk_gap == 1, chunk_gap == 2, chunk_gap == 3
    one = jnp.ones((1, B_DIM), F32)

    def by_chunk(rows):
        return jnp.concatenate([jnp.broadcast_to(x, (c, B_DIM)) for x in rows], axis=0)

    groups = range(ts // sub)
    rows = [slice(s * sub, (s + 1) * sub) for s in groups]
    decs = [dec_rows[s * per:(s + 1) * per] for s in groups]

    states = []
    for s in groups:
        d = decs[s]
        states.append(st.astype(BF16))
        to_end = by_chunk([d[1] * d[2] * d[3], d[2] * d[3], d[3], one])
        kv_t = _dot(v_t[:, rows[s]], (kend[rows[s]] * to_end).astype(BF16))
        st = st * (d[0] * d[1] * d[2] * d[3]) + kv_t

    scores = []
    for s in groups:
        d = decs[s]
        between1 = by_chunk([one, one, d[1], d[2]])
        between2 = by_chunk([one, one, one, d[1] * d[2]])
        qs = qd[rows[s]]
        same = _dot_nt(qs.astype(BF16), kinv[rows[s]])
        lhs = jnp.concatenate([qs, qs * between1, qs * between2], axis=0).astype(BF16)
        cross = _dot_nt(lhs, kend[rows[s]].astype(BF16))
        sc = jnp.where(causal, same,
                       jnp.where(gap1, cross[:sub],
                                 jnp.where(gap2, cross[sub:2 * sub],
                                           jnp.where(gap3, cross[2 * sub:], 0.0))))
        scores.append(sc.astype(BF16))

    outs = []
    for s in groups:
        d = decs[s]
        from_start = by_chunk([one, d[0], d[0] * d[1], d[0] * d[1] * d[2]])
        outs.append(_dot(scores[s], v16[rows[s]])
                    + _dot_nt((qd[rows[s]] * from_start).astype(BF16), states[s]))
    o = jnp.concatenate(outs, axis=0)

    o = o * lax.rsqrt(jnp.mean(o * o, axis=-1, keepdims=True) + EPS)
    return o * _silu(g), st


def _hgrn(proj, lb, batch, seq, *, ts=2048):
    n = proj.shape[1]
    nt = seq // ts
    base = (A_Q_W + 2 * A_KV_W) // LANES

    def col(kind):
        return pl.BlockSpec((None, ts, B_DIM), lambda b, h, t: (base + kind * B_HEADS + h, b * nt + t, 0))

    return pl.pallas_call(
        functools.partial(_hgrn_kernel, ts=ts),
        grid=(batch, B_HEADS, nt),
        in_specs=[pl.BlockSpec((1, B_DIM), lambda b, h, t: (0, h)), col(0), col(1), col(2), col(3)],
        out_specs=pl.BlockSpec((ts, B_DIM), lambda b, h, t: (b * nt + t, h)),
        out_shape=jax.ShapeDtypeStruct((n, B_W), BF16),
        scratch_shapes=[pltpu.VMEM((B_DIM, B_DIM), F32)],
        compiler_params=_cparams("parallel", "parallel", "arbitrary"),
        name="hgrn2",
    )(lb.reshape(1, B_W), proj, proj, proj, proj)


def _s5_rows(h_ref, nc):
    return jnp.concatenate(
        [h_ref[pl.ds(t, nc, stride=S5_T), :] for t in range(S5_T)], axis=1).astype(BF16)


S5_GPB = LANES // S5_GROUP


def _s5_sum_kernel(h_ref, w_ref, o_ref, wx_ref, *, rows):
    k = S5_T * LANES

    @pl.when(pl.program_id(1) == 0)
    def _():
        w = w_ref[0]
        row_group = (lax.broadcasted_iota(jnp.int32, (k, LANES), 0) % LANES) // S5_GROUP
        for g in range(S5_GPB):
            wx_ref[:, g * LANES:(g + 1) * LANES] = jnp.where(row_group == g, w, jnp.zeros_like(w))

    s = _dot(_s5_rows(h_ref, rows), wx_ref[...])
    for g in range(S5_GPB):
        o_ref[:, g, :] = s[:, g * LANES:(g + 1) * LANES]


def _s5_sum(h, w, batch, seq, *, nb=2):
    nc = seq // S5_T
    nblk, k, _ = w.shape
    return pl.pallas_call(
        functools.partial(_s5_sum_kernel, rows=nb * nc),
        grid=(nblk, batch // nb),
        in_specs=[pl.BlockSpec((nb * seq, LANES), lambda j, b: (b, j)),
                  pl.BlockSpec((1, k, LANES), lambda j, b: (j, 0, 0))],
        out_specs=pl.BlockSpec((nb * nc, S5_GPB, LANES), lambda j, b: (b, j, 0)),
        out_shape=jax.ShapeDtypeStruct((batch * nc, S5_GROUPS, LANES), F32),
        scratch_shapes=[pltpu.VMEM((k, S5_GPB * LANES), BF16)],
        compiler_params=_cparams("parallel", "arbitrary"),
        name="s5_chunk_sum",
    )(h, w)


def _s5_scan_kernel(s_ref, a1_ref, a2_ref, o_ref, x_ref, ss_ref, *, tc, nb):
    @pl.when(pl.program_id(0) == 0)
    def _():
        x_ref[...] = jnp.zeros_like(x_ref)

    a1 = a1_ref[...]
    a2 = a2_ref[...]

    rows, lanes = a1.shape
    for b in range(nb):
        blk = s_ref[b].reshape(tc * rows, lanes)
        ss_ref[b] = pltpu.roll(blk, S5_STATE, axis=1).reshape(tc, rows, lanes)

    def run(batches):
        def step(c, carry):
            new = []
            for b, (x, xs) in zip(batches, carry):
                o_ref[b, c] = x
                new.append((a1 * x + a2 * xs + s_ref[b, c], a1 * xs - a2 * x + ss_ref[b, c]))
            return tuple(new)

        init = tuple((x_ref[b], pltpu.roll(x_ref[b], S5_STATE, axis=1)) for b in batches)
        final = lax.fori_loop(0, tc, step, init, unroll=4)
        for b, (x, _) in zip(batches, final):
            x_ref[b] = x

    for b0 in range(0, nb, 2):
        run(range(b0, min(b0 + 2, nb)))


def _s5_scan(s, a1, a2, *, tc=32):
    nb, nc, rows, lanes = s.shape
    return pl.pallas_call(
        functools.partial(_s5_scan_kernel, tc=tc, nb=nb),
        grid=(nc // tc,),
        in_specs=[pl.BlockSpec((nb, tc, rows, lanes), lambda i: (0, i, 0, 0)),
                  pl.BlockSpec((rows, lanes), lambda i: (0, 0)),
                  pl.BlockSpec((rows, lanes), lambda i: (0, 0))],
        out_specs=pl.BlockSpec((nb, tc, rows, lanes), lambda i: (0, i, 0, 0)),
        out_shape=jax.ShapeDtypeStruct(s.shape, F32),
        scratch_shapes=[pltpu.VMEM((nb, rows, lanes), F32), pltpu.VMEM((nb, tc, rows, lanes), F32)],
        compiler_params=_cparams("arbitrary"),
        name="s5_chunk_scan",
    )(s, a1, a2)


def _s5_out_kernel(h_ref, x_ref, lag_ref, wc_ref, o_ref, wt_ref, *, nc):
    t = S5_T
    nq = 4
    qt = t // nq
    qw = qt * LANES

    @pl.when(pl.program_id(1) == 0)
    def _():
        zero = jnp.zeros((LANES, LANES), BF16)
        for tp in range(t):
            for s in range((tp // qt + 1) * qt):
                blk = lag_ref[0, tp - s] if tp >= s else zero
                wt_ref[s * LANES:(s + 1) * LANES, tp * LANES:(tp + 1) * LANES] = blk

    a = _s5_rows(h_ref, nc)
    xp = jnp.concatenate([x_ref[:, g, :] for g in range(S5_GPB)], axis=1).astype(BF16)
    for q in range(nq):
        cols = slice(q * qw, (q + 1) * qw)
        kq = (q + 1) * qw
        y = _dot(a[:, :kq], wt_ref[:kq, cols]) + _dot(xp, wc_ref[0, :, cols])
        for tt in range(qt):
            o_ref[pl.ds(q * qt + tt, nc, stride=t), :] = y[:, tt * LANES:(tt + 1) * LANES]


def _s5_out(h, xprev, w_lag, w_carry, batch, seq):
    nc = seq // S5_T
    nblk = w_lag.shape[0]
    kc = w_carry.shape[1]
    k = S5_T * LANES
    return pl.pallas_call(
        functools.partial(_s5_out_kernel, nc=nc),
        grid=(nblk, batch),
        in_specs=[pl.BlockSpec((seq, LANES), lambda j, b: (b, j)),
                  pl.BlockSpec((nc, S5_GPB, LANES), lambda j, b: (b, j, 0)),
                  pl.BlockSpec((1,) + w_lag.shape[1:], lambda j, b: (j, 0, 0, 0)),
                  pl.BlockSpec((1, kc, k), lambda j, b: (j, 0, 0))],
        out_specs=pl.BlockSpec((seq, LANES), lambda j, b: (b, j)),
        out_shape=jax.ShapeDtypeStruct(h.shape, F32),
        scratch_shapes=[pltpu.VMEM((k, k), BF16)],
        compiler_params=_cparams("parallel", "arbitrary"),
        name="s5_chunk_out",
    )(h, xprev, w_lag, w_carry)


def _s5_tables(a_re, a_im, log_step, b_re, b_im, c_re, c_im):
    hp = lax.Precision.HIGHEST
    t = S5_T
    step = jnp.exp(log_step)[:, None]
    mag = jnp.exp(step * a_re)
    ab_r = mag * jnp.cos(step * a_im)
    ab_i = mag * jnp.sin(step * a_im)
    den = a_re * a_re + a_im * a_im
    coef_r = ((ab_r - 1.0) * a_re + ab_i * a_im) / den
    coef_i = (ab_i * a_re - (ab_r - 1.0) * a_im) / den
    bb_r = coef_r[..., None] * b_re - coef_i[..., None] * b_im
    bb_i = coef_r[..., None] * b_im + coef_i[..., None] * b_re
    j = jnp.arange(t + 1, dtype=F32)[:, None, None]
    pmag = jnp.exp(j * (step * a_re)[None])
    pw_r = pmag * jnp.cos(j * (step * a_im)[None])
    pw_i = pmag * jnp.sin(j * (step * a_im)[None])

    rev_r, rev_i = pw_r[t - 1::-1], pw_i[t - 1::-1]
    ws_r = rev_r[:, :, :, None] * bb_r[None] - rev_i[:, :, :, None] * bb_i[None]
    ws_i = rev_r[:, :, :, None] * bb_i[None] + rev_i[:, :, :, None] * bb_r[None]
    w_sum = jnp.concatenate([ws_r, ws_i], axis=2)
    w_sum = w_sum.transpose(1, 0, 3, 2).reshape(S5_GROUPS, S5_ROW, 2 * S5_STATE)

    lb_r = pw_r[:t, :, :, None] * bb_r[None] - pw_i[:t, :, :, None] * bb_i[None]
    lb_i = pw_r[:t, :, :, None] * bb_i[None] + pw_i[:t, :, :, None] * bb_r[None]
    kmat = (jnp.einsum('gdp,jgpc->jgdc', c_re, lb_r, precision=hp)
            - jnp.einsum('gdp,jgpc->jgdc', c_im, lb_i, precision=hp))

    q_r, q_i = pw_r[1:], pw_i[1:]
    wc_r = c_re[None] * q_r[:, :, None, :] - c_im[None] * q_i[:, :, None, :]
    wc_i = -(c_re[None] * q_i[:, :, None, :] + c_im[None] * q_r[:, :, None, :])
    w_carry = jnp.concatenate([wc_r, wc_i], axis=3)
    w_carry = w_carry.transpose(1, 3, 0, 2).reshape(S5_GROUPS, 2 * S5_STATE, S5_ROW)

    a1 = jnp.concatenate([pw_r[t], pw_r[t]], axis=1)
    a2 = jnp.concatenate([-pw_i[t], pw_i[t]], axis=1)

    gl = S5_GPB
    nblk = S5_GROUPS // gl
    ws = w_sum.astype(BF16).reshape(nblk, gl, t, S5_GROUP, 2 * S5_STATE).transpose(0, 2, 1, 3, 4)
    ws = ws.reshape(nblk, t * LANES, 2 * S5_STATE)

    def block_diag(tbl, rows_per_group, reps):
        col = jnp.arange(reps * LANES)
        src = jnp.arange(reps * S5_GROUP)
        spread = ((col[None, :] // LANES == src[:, None] // S5_GROUP)
                  & (col[None, :] % S5_GROUP == src[:, None] % S5_GROUP)).astype(BF16)
        wide = jnp.einsum('...rc,cl->...rl', tbl.astype(BF16), spread, preferred_element_type=F32)
        row_group = (jnp.arange(tbl.shape[-2]) // rows_per_group) % gl
        col_group = (col % LANES) // S5_GROUP
        return jnp.where(row_group[:, None] == col_group[None, :], wide, 0.0).astype(BF16)

    lag = block_diag(kmat.transpose(0, 1, 3, 2).reshape(t, S5_GROUPS * S5_GROUP, S5_GROUP), S5_GROUP, 1)
    lag = lag.reshape(t, nblk, LANES, LANES).transpose(1, 0, 2, 3)
    wc = block_diag(w_carry.reshape(S5_GROUPS * 2 * S5_STATE, S5_ROW), 2 * S5_STATE, t)
    wc = wc.reshape(nblk, gl * 2 * S5_STATE, t * LANES)
    return ws, lag, wc, a1, a2


def _s5_ssm(h, tables, batch, seq):
    w_sum, w_lag, w_carry, a1, a2 = tables
    nc = seq // S5_T
    s = _s5_sum(h, w_sum, batch, seq)
    s4 = s.reshape(batch, nc, S5_GROUPS, 2 * S5_STATE)
    xprev = _s5_scan(s4, a1, a2).reshape(s.shape)
    return _s5_out(h, xprev, w_lag, w_carry, batch, seq)


def kernel(x, norm_g, ffn_w_in, ffn_w_out, mix_w_in, attn_sinks, hgrn_lb, mix_w_out, s5_a_re, s5_a_im, s5_log_step, s5_b_re, s5_b_im, s5_c_re, s5_c_im, s5_d, s5_w_glu, final_g):
    batch, seq, d = x.shape
    n = batch * seq
    x = x.reshape(n, d)

    w_in16 = ffn_w_in.astype(BF16)
    w_out16 = ffn_w_out.astype(BF16)
    mix_in16 = mix_w_in.astype(BF16)
    mix_out16 = mix_w_out.astype(BF16)
    glu16 = s5_w_glu.astype(BF16)
    fg = final_g.reshape(1, d)

    lb_p = jax.nn.softmax(hgrn_lb.astype(F32), axis=0)
    lb_all = jnp.cumsum(lb_p, axis=0) - lb_p[0]

    for layer in range(DEPTH):
        g3 = norm_g[layer].reshape(3, 1, d)
        if layer % 2 == 0:
            e = layer // 2
            x = _ffn(x, g3[0], w_in16, w_out16, layer, 0, g3[1])
            proj = _norm_proj(x, g3[1], mix_in16[e])
            attn = _attention(proj, attn_sinks[e], batch, seq)
            rec = _hgrn(proj, lb_all[e], batch, seq)
            pre, pre_args = "mix", (attn, rec, mix_out16, e)
        else:
            o = layer // 2
            tables = _s5_tables(s5_a_re[o], s5_a_im[o], s5_log_step[o], s5_b_re[o], s5_b_im[o],
                                s5_c_re[o], s5_c_im[o])
            x, h = _ffn(x, g3[0], w_in16, w_out16, layer, 0, g3[1], post="also_norm")
            y = _s5_ssm(h, tables, batch, seq)
            pre, pre_args = "s5", (h, y, s5_d[o].reshape(1, d), glu16, o)
        last = layer == DEPTH - 1
        x = _ffn(x, g3[2], w_in16, w_out16, layer, 1, fg, pre=pre, pre_args=pre_args,
                 post="norm_only" if last else "none")
    return x.reshape(batch, seq, d)
```

```python
import functools
import math

import jax
import jax.numpy as jnp
import numpy as np
from jax import lax
from jax.experimental import pallas as pl
from jax.experimental.pallas import tpu as pltpu

F32 = jnp.float32
BF16 = jnp.bfloat16

EPS = 1e-6
DEPTH = 4
D_MODEL = 1024
D_FF = 2816
A_HEADS = 8
A_KV_HEADS = 2
A_GROUP = A_HEADS // A_KV_HEADS
A_HEAD_DIM = 64
WINDOW = 128
A_Q_W = A_HEADS * A_HEAD_DIM
A_KV_W = A_KV_HEADS * A_HEAD_DIM
B_HEADS = 4
B_DIM = 128
B_CHUNK = 32
B_W = B_HEADS * B_DIM
MIX_IN = A_Q_W + 2 * A_KV_W + 4 * B_W
S5_GROUP = 16
S5_GROUPS = D_MODEL // S5_GROUP
S5_STATE = 64
S5_T = 16
S5_ROW = S5_T * S5_GROUP

LANES = 128
VMEM_LIMIT = 56 * 1024 * 1024


def _cparams(*sem):
    return pltpu.CompilerParams(dimension_semantics=sem, vmem_limit_bytes=VMEM_LIMIT)


def _sigmoid(x):
    return 1.0 / (1.0 + jnp.exp(-x))


def _silu(x):
    return x * _sigmoid(x)


def _rms(x, g):
    return x * lax.rsqrt(jnp.mean(x * x, axis=-1, keepdims=True) + EPS) * g


def _dot(a, b):
    return jnp.dot(a, b, preferred_element_type=F32)


def _dot_nt(a, b):
    return lax.dot_general(a, b, (((1,), (1,)), ((), ())), preferred_element_type=F32)


def _mix_residual(x_ref, a_ref, b_ref, w_ref):
    ka = a_ref.shape[1]
    return x_ref[...] + _dot(a_ref[...], w_ref[:ka, :]) + _dot(b_ref[...], w_ref[ka:, :])


def _s5_residual(x_ref, h_ref, y_ref, d_ref, w_ref):
    y = y_ref[...] + d_ref[...] * h_ref[...]
    inner = math.sqrt(2.0 / math.pi) * (y + 0.044715 * (y * y * y))
    ge = y * (0.5 * (1.0 + jnp.tanh(inner)))
    z = _dot(ge.astype(BF16), w_ref[...])
    d = x_ref.shape[1]
    return x_ref[...] + z[:, :d] * _sigmoid(z[:, d:])


_FFN_PRE = {"none": (1, lambda x_ref: x_ref[...]), "mix": (4, _mix_residual), "s5": (5, _s5_residual)}


def _ffn_kernel(*refs, tf, pre, post):
    n_pre, pre_fn = _FFN_PRE[pre]
    g_ref, wi_ref, wo_ref, g2_ref = refs[n_pre:n_pre + 4]
    o_refs = refs[n_pre + 4:]
    x = pre_fn(*refs[:n_pre])
    h = _rms(x, g_ref[...]).astype(BF16)
    acc = None
    for c in range(D_FF // tf):
        gate = _dot(h, wi_ref[:, c * tf:(c + 1) * tf])
        up = _dot(h, wi_ref[:, D_FF + c * tf:D_FF + (c + 1) * tf])
        act = (_silu(gate) * up).astype(BF16)
        part = _dot(act, wo_ref[c * tf:(c + 1) * tf, :])
        acc = part if acc is None else acc + part
    y = x + 0.5 * acc
    if post == "norm_only":
        o_refs[0][...] = _rms(y, g2_ref[...])
    else:
        o_refs[0][...] = y
        if post == "also_norm":
            o_refs[1][...] = _rms(y, g2_ref[...])


def _resident(shape, *index):
    lead = len(index)
    block = (None,) * lead + tuple(shape)
    full = tuple(index) + (0,) * len(shape)
    return pl.BlockSpec(block, lambda *_: full, pipeline_mode=pl.Buffered(1))


def _row_tile(stream_bytes_per_row, temp_bytes_per_row, resident_bytes):
    for tm in (1024, 512, 256):
        if resident_bytes + tm * (2 * stream_bytes_per_row + temp_bytes_per_row) <= VMEM_LIMIT:
            return tm
    raise ValueError("row tile does not fit VMEM")


def _ffn(x, g, w_in, w_out, layer, which, g2, *, pre="none", pre_args=(), post="none", tf=256):
    n, d = x.shape
    two = post == "also_norm"
    row_ops = [x] + {"mix": list(pre_args[:2]), "s5": list(pre_args[:2]), "none": []}[pre]
    stream = sum(a.shape[1] * a.dtype.itemsize for a in row_ops) + (2 if two else 1) * d * 4
    resident = sum(math.prod(w.shape[-2:]) * w.dtype.itemsize
                   for w in [w_in, w_out] + ([pre_args[-2]] if pre != "none" else []))
    temp = d * (2 + 4 + 4) + 2 * tf * (4 + 4 + 2) + (2 * d * 4 if pre == "s5" else 0)
    tm = _row_tile(stream, temp, resident)

    rows = lambda a: pl.BlockSpec((tm, a.shape[1]), lambda i: (i, 0))
    tile = pl.BlockSpec((tm, d), lambda i: (i, 0))
    if pre == "mix":
        a, b, w, idx = pre_args
        pre_ops, pre_specs = (a, b, w), [rows(a), rows(b), _resident(w.shape[1:], idx)]
    elif pre == "s5":
        h, y, dskip, w, idx = pre_args
        pre_ops = (h, y, dskip, w)
        pre_specs = [tile, tile, _resident(dskip.shape), _resident(w.shape[1:], idx)]
    else:
        pre_ops, pre_specs = (), []
    shape = jax.ShapeDtypeStruct((n, d), F32)
    return pl.pallas_call(
        functools.partial(_ffn_kernel, tf=tf, pre=pre, post=post),
        grid=(n // tm,),
        in_specs=[tile] + pre_specs + [
            _resident((1, d)),
            _resident((d, 2 * D_FF), layer, which),
            _resident((D_FF, d), layer, which),
            _resident((1, d)),
        ],
        out_specs=[tile, tile] if two else tile,
        out_shape=[shape, shape] if two else shape,
        compiler_params=_cparams("parallel"),
        name="ffn",
    )(x, *pre_ops, g, w_in, w_out, g2)


def _norm_proj_kernel(x_ref, g_ref, w_ref, o_ref):
    h = _rms(x_ref[...], g_ref[...]).astype(BF16)
    res = _dot(h, w_ref[...])
    for cb in range(o_ref.shape[0]):
        o_ref[cb] = res[:, cb * LANES:(cb + 1) * LANES]


def _norm_proj(x, g, w):
    n, d = x.shape
    m = w.shape[1]
    tm = _row_tile((d + m) * 4, d * 2 + m * 4, d * m * w.dtype.itemsize)
    return pl.pallas_call(
        _norm_proj_kernel,
        grid=(n // tm,),
        in_specs=[
            pl.BlockSpec((tm, d), lambda i: (i, 0)),
            _resident((1, d)),
            _resident((d, m)),
        ],
        out_specs=pl.BlockSpec((m // LANES, tm, LANES), lambda i: (0, i, 0)),
        out_shape=jax.ShapeDtypeStruct((m // LANES, n, LANES), F32),
        compiler_params=_cparams("parallel"),
        name="norm_proj",
    )(x, g, w)


def _attn_bias():
    w = WINDOW
    slopes = 2.0 ** (-8.0 * np.arange(1, A_HEADS + 1) / A_HEADS)
    dist = (np.arange(w) + w)[:, None] - np.arange(2 * w)[None, :]
    band = (dist >= 0) & (dist < w)
    has_prev = np.arange(2 * w)[None, :] >= w
    val = -slopes.reshape(A_KV_HEADS, A_GROUP, 1, 1) * dist.astype(np.float64)[None, None]
    out = np.stack([np.where(band & has_prev, val, -np.inf), np.where(band, val, -np.inf)])
    return jnp.asarray(out.reshape(2, A_KV_HEADS, A_GROUP * w, 2 * w), dtype=F32)


def _attn_kernel(sink_ref, bias_ref, q_ref, kc_ref, kp_ref, vc_ref, vp_ref, o_ref, *, nq):
    hk = pl.program_id(1)
    tile = pl.program_id(2)
    w = WINDOW
    dh = A_HEAD_DIM
    qw = A_GROUP * dh

    def widen(prev_ref, cur_ref):
        t = jnp.concatenate([prev_ref[...], cur_ref[...]], axis=0)
        r = pltpu.roll(t, dh, axis=1)
        lane = lax.broadcasted_iota(jnp.int32, t.shape, 1)
        t2 = jnp.where(lane // dh == hk, t, r)
        return jnp.concatenate([t2, t2], axis=1).astype(BF16)

    k4 = widen(kp_ref, kc_ref)
    v4 = widen(vp_ref, vc_ref)

    lane_head = lax.broadcasted_iota(jnp.int32, (w, qw), 1) // dh
    grp = lax.broadcasted_iota(jnp.int32, (A_GROUP * w, 1), 0) // w
    sink = jnp.zeros((A_GROUP * w, 1), F32)
    for g in range(A_GROUP):
        sink = jnp.where(grp == g, sink_ref[0, hk * A_GROUP + g], sink)

    bias_rest = bias_ref[1, 0]
    blocks = range(nq)
    keys = [slice(blk * w, (blk + 2) * w) for blk in blocks]

    scores = []
    for blk in blocks:
        q = jnp.concatenate([q_ref[c, blk * w:(blk + 1) * w, :] for c in range(qw // LANES)], axis=1)
        q = q * (dh ** -0.5)
        q4 = jnp.concatenate(
            [jnp.where(lane_head == g, q, 0.0) for g in range(A_GROUP)], axis=0).astype(BF16)
        bias = jnp.where(tile == 0, bias_ref[0, 0], bias_rest) if blk == 0 else bias_rest
        scores.append(_dot_nt(q4, k4[keys[blk]]) + bias)

    probs, denoms = [], []
    for s in scores:
        m = jnp.maximum(jnp.max(s, axis=-1, keepdims=True), sink)
        p = jnp.exp(s - m)
        denoms.append(jnp.sum(p, axis=-1, keepdims=True) + jnp.exp(sink - m))
        probs.append(p.astype(BF16))

    outs = [_dot(probs[blk], v4[keys[blk]]) / denoms[blk] for blk in blocks]

    for blk in blocks:
        o = jnp.zeros((w, qw), F32)
        for g in range(A_GROUP):
            o = o + jnp.where(lane_head == g, outs[blk][g * w:(g + 1) * w, :], 0.0)
        o_ref[blk * w:(blk + 1) * w, :] = o.astype(o_ref.dtype)


def _attention(proj, sinks, batch, seq, *, nq=8):
    n = proj.shape[1]
    rows = nq * WINDOW
    nt = seq // rows
    qw = A_GROUP * A_HEAD_DIM
    qblk = qw // LANES
    kcol = A_Q_W // LANES
    vcol = (A_Q_W + A_KV_W) // LANES
    cur = lambda b, h, i: b * nt + i
    prev = lambda b, h, i: b * nt * nq + jnp.maximum(i * nq - 1, 0)
    bias = _attn_bias()
    return pl.pallas_call(
        functools.partial(_attn_kernel, nq=nq),
        grid=(batch, A_KV_HEADS, nt),
        in_specs=[
            pl.BlockSpec(memory_space=pltpu.SMEM),
            pl.BlockSpec((2, 1) + bias.shape[2:], lambda b, h, i: (0, h, 0, 0)),
            pl.BlockSpec((qblk, rows, LANES), lambda b, h, i: (h, cur(b, h, i), 0)),
            pl.BlockSpec((None, rows, LANES), lambda b, h, i: (kcol, cur(b, h, i), 0)),
            pl.BlockSpec((None, WINDOW, LANES), lambda b, h, i: (kcol, prev(b, h, i), 0)),
            pl.BlockSpec((None, rows, LANES), lambda b, h, i: (vcol, cur(b, h, i), 0)),
            pl.BlockSpec((None, WINDOW, LANES), lambda b, h, i: (vcol, prev(b, h, i), 0)),
        ],
        out_specs=pl.BlockSpec((rows, qw), lambda b, h, i: (cur(b, h, i), h)),
        out_shape=jax.ShapeDtypeStruct((n, A_Q_W), BF16),
        compiler_params=_cparams("parallel", "parallel", "parallel"),
        name="swa",
    )(sinks.reshape(1, A_HEADS), bias, proj, proj, proj, proj, proj)


def _split3(x):
    hi = x.astype(BF16)
    r1 = x - hi.astype(F32)
    mid = r1.astype(BF16)
    lo = (r1 - mid.astype(F32)).astype(BF16)
    return hi, mid, lo


def _hgrn_kernel(lb_ref, q_ref, f_ref, i_ref, g_ref, o_ref, st_ref, *, ts):
    @pl.when(pl.program_id(2) == 0)
    def _():
        st_ref[...] = jnp.zeros_like(st_ref)

    out, st = _hgrn_head(lb_ref[...], q_ref[...], f_ref[...], i_ref[...], g_ref[...], st_ref[...], ts)
    st_ref[...] = st
    o_ref[...] = out.astype(o_ref.dtype)


def _hgrn_head(lb, q, f_logit, v32, g, st, ts):
    c = B_CHUNK
    nchunk = ts // c
    f = lb + (1.0 - lb) * _sigmoid(f_logit)
    k = 1.0 - f
    logf = jnp.log(f)

    sub = LANES
    r2 = lax.broadcasted_iota(jnp.int32, (sub, sub), 0)
    c2 = lax.broadcasted_iota(jnp.int32, (sub, sub), 1)
    causal = ((r2 // c) == (c2 // c)) & (c2 <= r2)

    tri = causal.astype(BF16)
    parts = _split3(logf)
    cum = jnp.concatenate(
        [sum(_dot(tri, p[s0:s0 + sub]) for p in parts) for s0 in range(0, ts, sub)], axis=0)
    tot_rows = [cum[(ch + 1) * c - 1:(ch + 1) * c, :] for ch in range(nchunk)]
    tot = jnp.concatenate([jnp.broadcast_to(t, (c, B_DIM)) for t in tot_rows], axis=0)
    dec_rows = [jnp.exp(t) for t in tot_rows]

    qd = _silu(q) * jnp.exp(cum)
    kinv = (k * jnp.exp(-cum)).astype(BF16)
    kend = k * jnp.exp(tot - cum)
    v16 = v32.astype(BF16)
    v_t = v32.T.astype(BF16)

    per = sub // c
    chunk_gap = r2 // c - c2 // c
    gap1, gap2, gap3 = chunk_gap == 1, chunk_gap == 2, chunk_gap == 3
    one = jnp.ones((1, B_DIM), F32)

    def by_chunk(rows):
        return jnp.concatenate([jnp.broadcast_to(x, (c, B_DIM)) for x in rows], axis=0)

    groups = range(ts // sub)
    rows = [slice(s * sub, (s + 1) * sub) for s in groups]
    decs = [dec_rows[s * per:(s + 1) * per] for s in groups]

    states = []
    for s in groups:
        d = decs[s]
        states.append(st.astype(BF16))
        to_end = by_chunk([d[1] * d[2] * d[3], d[2] * d[3], d[3], one])
        kv_t = _dot(v_t[:, rows[s]], (kend[rows[s]] * to_end).astype(BF16))
        st = st * (d[0] * d[1] * d[2] * d[3]) + kv_t

    scores = []
    for s in groups:
        d = decs[s]
        between1 = by_chunk([one, one, d[1], d[2]])
        between2 = by_chunk([one, one, one, d[1] * d[2]])
        qs = qd[rows[s]]
        same = _dot_nt(qs.astype(BF16), kinv[rows[s]])
        lhs = jnp.concatenate([qs, qs * between1, qs * between2], axis=0).astype(BF16)
        cross = _dot_nt(lhs, kend[rows[s]].astype(BF16))
        sc = jnp.where(causal, same,
                       jnp.where(gap1, cross[:sub],
                                 jnp.where(gap2, cross[sub:2 * sub],
                                           jnp.where(gap3, cross[2 * sub:], 0.0))))
        scores.append(sc.astype(BF16))

    outs = []
    for s in groups:
        d = decs[s]
        from_start = by_chunk([one, d[0], d[0] * d[1], d[0] * d[1] * d[2]])
        outs.append(_dot(scores[s], v16[rows[s]])
                    + _dot_nt((qd[rows[s]] * from_start).astype(BF16), states[s]))
    o = jnp.concatenate(outs, axis=0)

    o = o * lax.rsqrt(jnp.mean(o * o, axis=-1, keepdims=True) + EPS)
    return o * _silu(g), st


def _hgrn(proj, lb, batch, seq, *, ts=2048):
    n = proj.shape[1]
    nt = seq // ts
    base = (A_Q_W + 2 * A_KV_W) // LANES

    def col(kind):
        return pl.BlockSpec((None, ts, B_DIM), lambda b, h, t: (base + kind * B_HEADS + h, b * nt + t, 0))

    return pl.pallas_call(
        functools.partial(_hgrn_kernel, ts=ts),
        grid=(batch, B_HEADS, nt),
        in_specs=[pl.BlockSpec((1, B_DIM), lambda b, h, t: (0, h)), col(0), col(1), col(2), col(3)],
        out_specs=pl.BlockSpec((ts, B_DIM), lambda b, h, t: (b * nt + t, h)),
        out_shape=jax.ShapeDtypeStruct((n, B_W), BF16),
        scratch_shapes=[pltpu.VMEM((B_DIM, B_DIM), F32)],
        compiler_params=_cparams("parallel", "parallel", "arbitrary"),
        name="hgrn2",
    )(lb.reshape(1, B_W), proj, proj, proj, proj)


def _s5_rows(h_ref, nc):
    return jnp.concatenate(
        [h_ref[pl.ds(t, nc, stride=S5_T), :] for t in range(S5_T)], axis=1).astype(BF16)


S5_GPB = LANES // S5_GROUP


def _s5_sum_kernel(h_ref, w_ref, o_ref, wx_ref, *, rows):
    k = S5_T * LANES

    @pl.when(pl.program_id(1) == 0)
    def _():
        w = w_ref[0]
        row_group = (lax.broadcasted_iota(jnp.int32, (k, LANES), 0) % LANES) // S5_GROUP
        for g in range(S5_GPB):
            wx_ref[:, g * LANES:(g + 1) * LANES] = jnp.where(row_group == g, w, jnp.zeros_like(w))

    s = _dot(_s5_rows(h_ref, rows), wx_ref[...])
    for g in range(S5_GPB):
        o_ref[:, g, :] = s[:, g * LANES:(g + 1) * LANES]


def _s5_sum(h, w, batch, seq, *, nb=2):
    nc = seq // S5_T
    nblk, k, _ = w.shape
    return pl.pallas_call(
        functools.partial(_s5_sum_kernel, rows=nb * nc),
        grid=(nblk, batch // nb),
        in_specs=[pl.BlockSpec((nb * seq, LANES), lambda j, b: (b, j)),
                  pl.BlockSpec((1, k, LANES), lambda j, b: (j, 0, 0))],
        out_specs=pl.BlockSpec((nb * nc, S5_GPB, LANES), lambda j, b: (b, j, 0)),
        out_shape=jax.ShapeDtypeStruct((batch * nc, S5_GROUPS, LANES), F32),
        scratch_shapes=[pltpu.VMEM((k, S5_GPB * LANES), BF16)],
        compiler_params=_cparams("parallel", "arbitrary"),
        name="s5_chunk_sum",
    )(h, w)


def _s5_scan_kernel(s_ref, a1_ref, a2_ref, o_ref, x_ref, ss_ref, *, tc, nb):
    @pl.when(pl.program_id(0) == 0)
    def _():
        x_ref[...] = jnp.zeros_like(x_ref)

    a1 = a1_ref[...]
    a2 = a2_ref[...]

    rows, lanes = a1.shape
    for b in range(nb):
        blk = s_ref[b].reshape(tc * rows, lanes)
        ss_ref[b] = pltpu.roll(blk, S5_STATE, axis=1).reshape(tc, rows, lanes)

    def run(batches):
        def step(c, carry):
            new = []
            for b, (x, xs) in zip(batches, carry):
                o_ref[b, c] = x
                new.append((a1 * x + a2 * xs + s_ref[b, c], a1 * xs - a2 * x + ss_ref[b, c]))
            return tuple(new)

        init = tuple((x_ref[b], pltpu.roll(x_ref[b], S5_STATE, axis=1)) for b in batches)
        final = lax.fori_loop(0, tc, step, init, unroll=4)
        for b, (x, _) in zip(batches, final):
            x_ref[b] = x

    for b0 in range(0, nb, 2):
        run(range(b0, min(b0 + 2, nb)))


def _s5_scan(s, a1, a2, *, tc=64):
    nb, nc, rows, lanes = s.shape
    return pl.pallas_call(
        functools.partial(_s5_scan_kernel, tc=tc, nb=nb),
        grid=(nc // tc,),
        in_specs=[pl.BlockSpec((nb, tc, rows, lanes), lambda i: (0, i, 0, 0)),
                  pl.BlockSpec((rows, lanes), lambda i: (0, 0)),
                  pl.BlockSpec((rows, lanes), lambda i: (0, 0))],
        out_specs=pl.BlockSpec((nb, tc, rows, lanes), lambda i: (0, i, 0, 0)),
        out_shape=jax.ShapeDtypeStruct(s.shape, F32),
        scratch_shapes=[pltpu.VMEM((nb, rows, lanes), F32), pltpu.VMEM((nb, tc, rows, lanes), F32)],
        compiler_params=_cparams("arbitrary"),
        name="s5_chunk_scan",
    )(s, a1, a2)


def _s5_spread():
    src = np.arange(S5_ROW)
    col = np.arange(S5_T * LANES)
    hit = (src[:, None] // S5_GROUP == col[None, :] // LANES) & (src[:, None] % S5_GROUP == col[None, :] % S5_GROUP)
    return jnp.asarray(hit, dtype=BF16)


def _s5_out_kernel(h_ref, x_ref, lag_ref, wcc_ref, spread_ref, o_ref, wt_ref, wc_ref, *, nc):
    t = S5_T
    nq = 4
    qt = t // nq
    qw = qt * LANES

    @pl.when(pl.program_id(1) == 0)
    def _():
        in_group = (lax.broadcasted_iota(jnp.int32, (LANES, LANES), 0) // S5_GROUP
                    == lax.broadcasted_iota(jnp.int32, (LANES, LANES), 1) // S5_GROUP)
        lags = [jnp.where(in_group, _dot(lag_ref[0, lg], spread_ref[:S5_GROUP, :LANES]), 0.0).astype(BF16)
                for lg in range(t)]
        zero = jnp.zeros((LANES, LANES), BF16)
        for tp in range(t):
            for s in range((tp // qt + 1) * qt):
                blk = lags[tp - s] if tp >= s else zero
                wt_ref[s * LANES:(s + 1) * LANES, tp * LANES:(tp + 1) * LANES] = blk
        kc = wc_ref.shape[0]
        row_group = lax.broadcasted_iota(jnp.int32, (kc, qw), 0) // (2 * S5_STATE)
        col_group = (lax.broadcasted_iota(jnp.int32, (kc, qw), 1) % LANES) // S5_GROUP
        for q in range(nq):
            cols = slice(q * qw, (q + 1) * qw)
            wide = _dot(wcc_ref[0], spread_ref[:, cols])
            wc_ref[:, cols] = jnp.where(row_group == col_group, wide, 0.0).astype(BF16)

    a = _s5_rows(h_ref, nc)
    xp = jnp.concatenate([x_ref[:, g, :] for g in range(S5_GPB)], axis=1).astype(BF16)
    for q in range(nq):
        cols = slice(q * qw, (q + 1) * qw)
        kq = (q + 1) * qw
        y = _dot(a[:, :kq], wt_ref[:kq, cols]) + _dot(xp, wc_ref[:, cols])
        for tt in range(qt):
            o_ref[pl.ds(q * qt + tt, nc, stride=t), :] = y[:, tt * LANES:(tt + 1) * LANES]


def _s5_out(h, xprev, w_lag, w_carry, batch, seq):
    nc = seq // S5_T
    nblk = w_lag.shape[0]
    kc = w_carry.shape[1]
    k = S5_T * LANES
    return pl.pallas_call(
        functools.partial(_s5_out_kernel, nc=nc),
        grid=(nblk, batch),
        in_specs=[pl.BlockSpec((seq, LANES), lambda j, b: (b, j)),
                  pl.BlockSpec((nc, S5_GPB, LANES), lambda j, b: (b, j, 0)),
                  pl.BlockSpec((1,) + w_lag.shape[1:], lambda j, b: (j, 0, 0, 0)),
                  pl.BlockSpec((1, kc, S5_ROW), lambda j, b: (j, 0, 0)),
                  _resident((S5_ROW, k))],
        out_specs=pl.BlockSpec((seq, LANES), lambda j, b: (b, j)),
        out_shape=jax.ShapeDtypeStruct(h.shape, F32),
        scratch_shapes=[pltpu.VMEM((k, k), BF16), pltpu.VMEM((kc, k), BF16)],
        compiler_params=_cparams("parallel", "arbitrary"),
        name="s5_chunk_out",
    )(h, xprev, w_lag, w_carry, _s5_spread())


def _s5_tables(a_re, a_im, log_step, b_re, b_im, c_re, c_im):
    hp = lax.Precision.HIGHEST
    t = S5_T
    step = jnp.exp(log_step)[:, None]
    mag = jnp.exp(step * a_re)
    ab_r = mag * jnp.cos(step * a_im)
    ab_i = mag * jnp.sin(step * a_im)
    den = a_re * a_re + a_im * a_im
    coef_r = ((ab_r - 1.0) * a_re + ab_i * a_im) / den
    coef_i = (ab_i * a_re - (ab_r - 1.0) * a_im) / den
    bb_r = coef_r[..., None] * b_re - coef_i[..., None] * b_im
    bb_i = coef_r[..., None] * b_im + coef_i[..., None] * b_re
    j = jnp.arange(t + 1, dtype=F32)[:, None, None]
    pmag = jnp.exp(j * (step * a_re)[None])
    pw_r = pmag * jnp.cos(j * (step * a_im)[None])
    pw_i = pmag * jnp.sin(j * (step * a_im)[None])

    rev_r, rev_i = pw_r[t - 1::-1], pw_i[t - 1::-1]
    ws_r = rev_r[:, :, :, None] * bb_r[None] - rev_i[:, :, :, None] * bb_i[None]
    ws_i = rev_r[:, :, :, None] * bb_i[None] + rev_i[:, :, :, None] * bb_r[None]
    w_sum = jnp.concatenate([ws_r, ws_i], axis=2)
    w_sum = w_sum.transpose(1, 0, 3, 2).reshape(S5_GROUPS, S5_ROW, 2 * S5_STATE)

    lb_r = pw_r[:t, :, :, None] * bb_r[None] - pw_i[:t, :, :, None] * bb_i[None]
    lb_i = pw_r[:t, :, :, None] * bb_i[None] + pw_i[:t, :, :, None] * bb_r[None]
    kmat = (jnp.einsum('gdp,jgpc->jgdc', c_re, lb_r, precision=hp)
            - jnp.einsum('gdp,jgpc->jgdc', c_im, lb_i, precision=hp))

    q_r, q_i = pw_r[1:], pw_i[1:]
    wc_r = c_re[None] * q_r[:, :, None, :] - c_im[None] * q_i[:, :, None, :]
    wc_i = -(c_re[None] * q_i[:, :, None, :] + c_im[None] * q_r[:, :, None, :])
    w_carry = jnp.concatenate([wc_r, wc_i], axis=3)
    w_carry = w_carry.transpose(1, 3, 0, 2).reshape(S5_GROUPS, 2 * S5_STATE, S5_ROW)

    a1 = jnp.concatenate([pw_r[t], pw_r[t]], axis=1)
    a2 = jnp.concatenate([-pw_i[t], pw_i[t]], axis=1)

    gl = S5_GPB
    nblk = S5_GROUPS // gl
    ws = w_sum.astype(BF16).reshape(nblk, gl, t, S5_GROUP, 2 * S5_STATE).transpose(0, 2, 1, 3, 4)
    ws = ws.reshape(nblk, t * LANES, 2 * S5_STATE)

    lag = kmat.transpose(0, 1, 3, 2).reshape(t, nblk, LANES, S5_GROUP).astype(BF16)
    lag = lag.transpose(1, 0, 2, 3)
    wc = w_carry.astype(BF16).reshape(nblk, gl * 2 * S5_STATE, S5_ROW)
    return ws, lag, wc, a1, a2


def _s5_ssm(h, tables, batch, seq):
    w_sum, w_lag, w_carry, a1, a2 = tables
    nc = seq // S5_T
    s = _s5_sum(h, w_sum, batch, seq)
    s4 = s.reshape(batch, nc, S5_GROUPS, 2 * S5_STATE)
    xprev = _s5_scan(s4, a1, a2).reshape(s.shape)
    return _s5_out(h, xprev, w_lag, w_carry, batch, seq)


def kernel(x, norm_g, ffn_w_in, ffn_w_out, mix_w_in, attn_sinks, hgrn_lb, mix_w_out, s5_a_re, s5_a_im, s5_log_step, s5_b_re, s5_b_im, s5_c_re, s5_c_im, s5_d, s5_w_glu, final_g):
    batch, seq, d = x.shape
    n = batch * seq
    x = x.reshape(n, d)

    w_in16 = ffn_w_in.astype(BF16)
    w_out16 = ffn_w_out.astype(BF16)
    mix_in16 = mix_w_in.astype(BF16)
    mix_out16 = mix_w_out.astype(BF16)
    glu16 = s5_w_glu.astype(BF16)
    fg = final_g.reshape(1, d)

    lb_p = jax.nn.softmax(hgrn_lb.astype(F32), axis=0)
    lb_all = jnp.cumsum(lb_p, axis=0) - lb_p[0]

    for layer in range(DEPTH):
        g3 = norm_g[layer].reshape(3, 1, d)
        if layer % 2 == 0:
            e = layer // 2
            x = _ffn(x, g3[0], w_in16, w_out16, layer, 0, g3[1])
            proj = _norm_proj(x, g3[1], mix_in16[e])
            attn = _attention(proj, attn_sinks[e], batch, seq)
            rec = _hgrn(proj, lb_all[e], batch, seq)
            pre, pre_args = "mix", (attn, rec, mix_out16, e)
        else:
            o = layer // 2
            tables = _s5_tables(s5_a_re[o], s5_a_im[o], s5_log_step[o], s5_b_re[o], s5_b_im[o],
                                s5_c_re[o], s5_c_im[o])
            x, h = _ffn(x, g3[0], w_in16, w_out16, layer, 0, g3[1], post="also_norm")
            y = _s5_ssm(h, tables, batch, seq)
            pre, pre_args = "s5", (h, y, s5_d[o].reshape(1, d), glu16, o)
        last = layer == DEPTH - 1
        x = _ffn(x, g3[2], w_in16, w_out16, layer, 1, fg, pre=pre, pre_args=pre_args,
                 post="norm_only" if last else "none")
    return x.reshape(batch, seq, d)
```

```python
import functools
import math

import jax
import jax.numpy as jnp
import numpy as np
from jax import lax
from jax.experimental import pallas as pl
from jax.experimental.pallas import tpu as pltpu

F32 = jnp.float32
BF16 = jnp.bfloat16

EPS = 1e-6
DEPTH = 4
D_MODEL = 1024
D_FF = 2816
A_HEADS = 8
A_KV_HEADS = 2
A_GROUP = A_HEADS // A_KV_HEADS
A_HEAD_DIM = 64
WINDOW = 128
A_Q_W = A_HEADS * A_HEAD_DIM
A_KV_W = A_KV_HEADS * A_HEAD_DIM
B_HEADS = 4
B_DIM = 128
B_CHUNK = 32
B_W = B_HEADS * B_DIM
MIX_IN = A_Q_W + 2 * A_KV_W + 4 * B_W
S5_GROUP = 16
S5_GROUPS = D_MODEL // S5_GROUP
S5_STATE = 64
S5_T = 16
S5_ROW = S5_T * S5_GROUP

LANES = 128
VMEM_LIMIT = 56 * 1024 * 1024


def _cparams(*sem):
    return pltpu.CompilerParams(dimension_semantics=sem, vmem_limit_bytes=VMEM_LIMIT)


def _sigmoid(x):
    return 1.0 / (1.0 + jnp.exp(-x))


def _silu(x):
    return x * _sigmoid(x)


def _rms(x, g):
    return x * lax.rsqrt(jnp.mean(x * x, axis=-1, keepdims=True) + EPS) * g


def _dot(a, b):
    return jnp.dot(a, b, preferred_element_type=F32)


def _dot_nt(a, b):
    return lax.dot_general(a, b, (((1,), (1,)), ((), ())), preferred_element_type=F32)


def _mix_residual(x_ref, a_ref, b_ref, w_ref):
    ka = a_ref.shape[1]
    return x_ref[...] + _dot(a_ref[...], w_ref[:ka, :]) + _dot(b_ref[...], w_ref[ka:, :])


def _s5_residual(x_ref, h_ref, y_ref, d_ref, w_ref):
    y = y_ref[...] + d_ref[...] * h_ref[...]
    inner = math.sqrt(2.0 / math.pi) * (y + 0.044715 * (y * y * y))
    ge = y * (0.5 * (1.0 + jnp.tanh(inner)))
    z = _dot(ge.astype(BF16), w_ref[...])
    d = x_ref.shape[1]
    return x_ref[...] + z[:, :d] * _sigmoid(z[:, d:])


_FFN_PRE = {"none": (1, lambda x_ref: x_ref[...]), "mix": (4, _mix_residual), "s5": (5, _s5_residual)}


def _ffn_kernel(*refs, tf, pre, post):
    n_pre, pre_fn = _FFN_PRE[pre]
    g_ref, wi_ref, wo_ref, g2_ref = refs[n_pre:n_pre + 4]
    o_refs = refs[n_pre + 4:]
    x = pre_fn(*refs[:n_pre])
    h = _rms(x, g_ref[...]).astype(BF16)
    acc = None
    for c in range(D_FF // tf):
        gate = _dot(h, wi_ref[:, c * tf:(c + 1) * tf])
        up = _dot(h, wi_ref[:, D_FF + c * tf:D_FF + (c + 1) * tf])
        act = (_silu(gate) * up).astype(BF16)
        part = _dot(act, wo_ref[c * tf:(c + 1) * tf, :])
        acc = part if acc is None else acc + part
    y = x + 0.5 * acc
    if post == "norm_only":
        o_refs[0][...] = _rms(y, g2_ref[...])
    else:
        o_refs[0][...] = y
        if post == "also_norm":
            o_refs[1][...] = _rms(y, g2_ref[...])


def _resident(shape, *index):
    lead = len(index)
    block = (None,) * lead + tuple(shape)
    full = tuple(index) + (0,) * len(shape)
    return pl.BlockSpec(block, lambda *_: full, pipeline_mode=pl.Buffered(1))


def _row_tile(stream_bytes_per_row, temp_bytes_per_row, resident_bytes):
    for tm in (1024, 512, 256):
        if resident_bytes + tm * (2 * stream_bytes_per_row + temp_bytes_per_row) <= VMEM_LIMIT:
            return tm
    raise ValueError("row tile does not fit VMEM")


def _ffn(x, g, w_in, w_out, layer, which, g2, *, pre="none", pre_args=(), post="none", tf=256):
    n, d = x.shape
    two = post == "also_norm"
    row_ops = [x] + {"mix": list(pre_args[:2]), "s5": list(pre_args[:2]), "none": []}[pre]
    stream = sum(a.shape[1] * a.dtype.itemsize for a in row_ops) + (2 if two else 1) * d * 4
    resident = sum(math.prod(w.shape[-2:]) * w.dtype.itemsize
                   for w in [w_in, w_out] + ([pre_args[-2]] if pre != "none" else []))
    temp = d * (2 + 4 + 4) + 2 * tf * (4 + 4 + 2) + (2 * d * 4 if pre == "s5" else 0)
    tm = _row_tile(stream, temp, resident)

    rows = lambda a: pl.BlockSpec((tm, a.shape[1]), lambda i: (i, 0))
    tile = pl.BlockSpec((tm, d), lambda i: (i, 0))
    if pre == "mix":
        a, b, w, idx = pre_args
        pre_ops, pre_specs = (a, b, w), [rows(a), rows(b), _resident(w.shape[1:], idx)]
    elif pre == "s5":
        h, y, dskip, w, idx = pre_args
        pre_ops = (h, y, dskip, w)
        pre_specs = [tile, tile, _resident(dskip.shape), _resident(w.shape[1:], idx)]
    else:
        pre_ops, pre_specs = (), []
    shape = jax.ShapeDtypeStruct((n, d), F32)
    return pl.pallas_call(
        functools.partial(_ffn_kernel, tf=tf, pre=pre, post=post),
        grid=(n // tm,),
        in_specs=[tile] + pre_specs + [
            _resident((1, d)),
            _resident((d, 2 * D_FF), layer, which),
            _resident((D_FF, d), layer, which),
            _resident((1, d)),
        ],
        out_specs=[tile, tile] if two else tile,
        out_shape=[shape, shape] if two else shape,
        compiler_params=_cparams("parallel"),
        name="ffn",
    )(x, *pre_ops, g, w_in, w_out, g2)


def _norm_proj_kernel(x_ref, g_ref, w_ref, o_ref):
    h = _rms(x_ref[...], g_ref[...]).astype(BF16)
    res = _dot(h, w_ref[...])
    for cb in range(o_ref.shape[0]):
        o_ref[cb] = res[:, cb * LANES:(cb + 1) * LANES]


def _norm_proj(x, g, w):
    n, d = x.shape
    m = w.shape[1]
    tm = _row_tile((d + m) * 4, d * 2 + m * 4, d * m * w.dtype.itemsize)
    return pl.pallas_call(
        _norm_proj_kernel,
        grid=(n // tm,),
        in_specs=[
            pl.BlockSpec((tm, d), lambda i: (i, 0)),
            _resident((1, d)),
            _resident((d, m)),
        ],
        out_specs=pl.BlockSpec((m // LANES, tm, LANES), lambda i: (0, i, 0)),
        out_shape=jax.ShapeDtypeStruct((m // LANES, n, LANES), F32),
        compiler_params=_cparams("parallel"),
        name="norm_proj",
    )(x, g, w)


def _attn_bias():
    w = WINDOW
    slopes = 2.0 ** (-8.0 * np.arange(1, A_HEADS + 1) / A_HEADS)
    dist = (np.arange(w) + w)[:, None] - np.arange(2 * w)[None, :]
    band = (dist >= 0) & (dist < w)
    has_prev = np.arange(2 * w)[None, :] >= w
    val = -slopes.reshape(A_KV_HEADS, A_GROUP, 1, 1) * dist.astype(np.float64)[None, None]
    out = np.stack([np.where(band & has_prev, val, -np.inf), np.where(band, val, -np.inf)])
    return jnp.asarray(out.reshape(2, A_KV_HEADS, A_GROUP * w, 2 * w), dtype=F32)


def _attn_kernel(sink_ref, bias_ref, q_ref, kc_ref, kp_ref, vc_ref, vp_ref, o_ref, *, nq):
    hk = pl.program_id(1)
    tile = pl.program_id(2)
    w = WINDOW
    dh = A_HEAD_DIM
    qw = A_GROUP * dh

    def widen(prev_ref, cur_ref):
        t = jnp.concatenate([prev_ref[...], cur_ref[...]], axis=0)
        r = pltpu.roll(t, dh, axis=1)
        lane = lax.broadcasted_iota(jnp.int32, t.shape, 1)
        t2 = jnp.where(lane // dh == hk, t, r)
        return jnp.concatenate([t2, t2], axis=1).astype(BF16)

    k4 = widen(kp_ref, kc_ref)
    v4 = widen(vp_ref, vc_ref)

    lane_head = lax.broadcasted_iota(jnp.int32, (w, qw), 1) // dh
    grp = lax.broadcasted_iota(jnp.int32, (A_GROUP * w, 1), 0) // w
    sink = jnp.zeros((A_GROUP * w, 1), F32)
    for g in range(A_GROUP):
        sink = jnp.where(grp == g, sink_ref[0, hk * A_GROUP + g], sink)

    bias_rest = bias_ref[1, 0]
    blocks = range(nq)
    keys = [slice(blk * w, (blk + 2) * w) for blk in blocks]

    scores = []
    for blk in blocks:
        q = jnp.concatenate([q_ref[c, blk * w:(blk + 1) * w, :] for c in range(qw // LANES)], axis=1)
        q = q * (dh ** -0.5)
        q4 = jnp.concatenate(
            [jnp.where(lane_head == g, q, 0.0) for g in range(A_GROUP)], axis=0).astype(BF16)
        bias = jnp.where(tile == 0, bias_ref[0, 0], bias_rest) if blk == 0 else bias_rest
        scores.append(_dot_nt(q4, k4[keys[blk]]) + bias)

    probs, denoms = [], []
    for s in scores:
        m = jnp.maximum(jnp.max(s, axis=-1, keepdims=True), sink)
        p = jnp.exp(s - m)
        denoms.append(jnp.sum(p, axis=-1, keepdims=True) + jnp.exp(sink - m))
        probs.append(p.astype(BF16))

    outs = [_dot(probs[blk], v4[keys[blk]]) / denoms[blk] for blk in blocks]

    for blk in blocks:
        o = jnp.zeros((w, qw), F32)
        for g in range(A_GROUP):
            o = o + jnp.where(lane_head == g, outs[blk][g * w:(g + 1) * w, :], 0.0)
        o_ref[blk * w:(blk + 1) * w, :] = o.astype(o_ref.dtype)


def _attention(proj, sinks, batch, seq, *, nq=16):
    n = proj.shape[1]
    rows = nq * WINDOW
    nt = seq // rows
    qw = A_GROUP * A_HEAD_DIM
    qblk = qw // LANES
    kcol = A_Q_W // LANES
    vcol = (A_Q_W + A_KV_W) // LANES
    cur = lambda b, h, i: b * nt + i
    prev = lambda b, h, i: b * nt * nq + jnp.maximum(i * nq - 1, 0)
    bias = _attn_bias()
    return pl.pallas_call(
        functools.partial(_attn_kernel, nq=nq),
        grid=(batch, A_KV_HEADS, nt),
        in_specs=[
            pl.BlockSpec(memory_space=pltpu.SMEM),
            pl.BlockSpec((2, 1) + bias.shape[2:], lambda b, h, i: (0, h, 0, 0)),
            pl.BlockSpec((qblk, rows, LANES), lambda b, h, i: (h, cur(b, h, i), 0)),
            pl.BlockSpec((None, rows, LANES), lambda b, h, i: (kcol, cur(b, h, i), 0)),
            pl.BlockSpec((None, WINDOW, LANES), lambda b, h, i: (kcol, prev(b, h, i), 0)),
            pl.BlockSpec((None, rows, LANES), lambda b, h, i: (vcol, cur(b, h, i), 0)),
            pl.BlockSpec((None, WINDOW, LANES), lambda b, h, i: (vcol, prev(b, h, i), 0)),
        ],
        out_specs=pl.BlockSpec((rows, qw), lambda b, h, i: (cur(b, h, i), h)),
        out_shape=jax.ShapeDtypeStruct((n, A_Q_W), BF16),
        compiler_params=_cparams("parallel", "parallel", "parallel"),
        name="swa",
    )(sinks.reshape(1, A_HEADS), bias, proj, proj, proj, proj, proj)


def _split3(x):
    hi = x.astype(BF16)
    r1 = x - hi.astype(F32)
    mid = r1.astype(BF16)
    lo = (r1 - mid.astype(F32)).astype(BF16)
    return hi, mid, lo


def _hgrn_kernel(lb_ref, q_ref, f_ref, i_ref, g_ref, o_ref, st_ref, *, ts):
    @pl.when(pl.program_id(2) == 0)
    def _():
        st_ref[...] = jnp.zeros_like(st_ref)

    out, st = _hgrn_head(lb_ref[...], q_ref[...], f_ref[...], i_ref[...], g_ref[...], st_ref[...], ts)
    st_ref[...] = st
    o_ref[...] = out.astype(o_ref.dtype)


def _hgrn_head(lb, q, f_logit, v32, g, st, ts):
    c = B_CHUNK
    nchunk = ts // c
    f = lb + (1.0 - lb) * _sigmoid(f_logit)
    k = 1.0 - f
    logf = jnp.log(f)

    sub = LANES
    r2 = lax.broadcasted_iota(jnp.int32, (sub, sub), 0)
    c2 = lax.broadcasted_iota(jnp.int32, (sub, sub), 1)
    causal = ((r2 // c) == (c2 // c)) & (c2 <= r2)

    tri = causal.astype(BF16)
    parts = _split3(logf)
    cum = jnp.concatenate(
        [sum(_dot(tri, p[s0:s0 + sub]) for p in parts) for s0 in range(0, ts, sub)], axis=0)
    tot_rows = [cum[(ch + 1) * c - 1:(ch + 1) * c, :] for ch in range(nchunk)]
    tot = jnp.concatenate([jnp.broadcast_to(t, (c, B_DIM)) for t in tot_rows], axis=0)
    dec_rows = [jnp.exp(t) for t in tot_rows]

    qd = _silu(q) * jnp.exp(cum)
    kinv = (k * jnp.exp(-cum)).astype(BF16)
    kend = k * jnp.exp(tot - cum)
    v16 = v32.astype(BF16)
    v_t = v32.T.astype(BF16)

    per = sub // c
    chunk_gap = r2 // c - c2 // c
    gap1, gap2, gap3 = chunk_gap == 1, chunk_gap == 2, chunk_gap == 3
    one = jnp.ones((1, B_DIM), F32)

    def by_chunk(rows):
        return jnp.concatenate([jnp.broadcast_to(x, (c, B_DIM)) for x in rows], axis=0)

    groups = range(ts // sub)
    rows = [slice(s * sub, (s + 1) * sub) for s in groups]
    decs = [dec_rows[s * per:(s + 1) * per] for s in groups]

    states = []
    for s in groups:
        d = decs[s]
        states.append(st.astype(BF16))
        to_end = by_chunk([d[1] * d[2] * d[3], d[2] * d[3], d[3], one])
        kv_t = _dot(v_t[:, rows[s]], (kend[rows[s]] * to_end).astype(BF16))
        st = st * (d[0] * d[1] * d[2] * d[3]) + kv_t

    scores = []
    for s in groups:
        d = decs[s]
        between1 = by_chunk([one, one, d[1], d[2]])
        between2 = by_chunk([one, one, one, d[1] * d[2]])
        qs = qd[rows[s]]
        same = _dot_nt(qs.astype(BF16), kinv[rows[s]])
        lhs = jnp.concatenate([qs, qs * between1, qs * between2], axis=0).astype(BF16)
        cross = _dot_nt(lhs, kend[rows[s]].astype(BF16))
        sc = jnp.where(causal, same,
                       jnp.where(gap1, cross[:sub],
                                 jnp.where(gap2, cross[sub:2 * sub],
                                           jnp.where(gap3, cross[2 * sub:], 0.0))))
        scores.append(sc.astype(BF16))

    outs = []
    for s in groups:
        d = decs[s]
        from_start = by_chunk([one, d[0], d[0] * d[1], d[0] * d[1] * d[2]])
        outs.append(_dot(scores[s], v16[rows[s]])
                    + _dot_nt((qd[rows[s]] * from_start).astype(BF16), states[s]))
    o = jnp.concatenate(outs, axis=0)

    o = o * lax.rsqrt(jnp.mean(o * o, axis=-1, keepdims=True) + EPS)
    return o * _silu(g), st


def _hgrn(proj, lb, batch, seq, *, ts=4096):
    n = proj.shape[1]
    nt = seq // ts
    base = (A_Q_W + 2 * A_KV_W) // LANES

    def col(kind):
        return pl.BlockSpec((None, ts, B_DIM), lambda b, h, t: (base + kind * B_HEADS + h, b * nt + t, 0))

    return pl.pallas_call(
        functools.partial(_hgrn_kernel, ts=ts),
        grid=(batch, B_HEADS, nt),
        in_specs=[pl.BlockSpec((1, B_DIM), lambda b, h, t: (0, h)), col(0), col(1), col(2), col(3)],
        out_specs=pl.BlockSpec((ts, B_DIM), lambda b, h, t: (b * nt + t, h)),
        out_shape=jax.ShapeDtypeStruct((n, B_W), BF16),
        scratch_shapes=[pltpu.VMEM((B_DIM, B_DIM), F32)],
        compiler_params=_cparams("parallel", "parallel", "arbitrary"),
        name="hgrn2",
    )(lb.reshape(1, B_W), proj, proj, proj, proj)


def _s5_rows(h_ref, nc):
    return jnp.concatenate(
        [h_ref[pl.ds(t, nc, stride=S5_T), :] for t in range(S5_T)], axis=1).astype(BF16)


S5_GPB = LANES // S5_GROUP


def _s5_sum_kernel(h_ref, w_ref, o_ref, wx_ref, *, rows):
    k = S5_T * LANES

    @pl.when(pl.program_id(1) == 0)
    def _():
        w = w_ref[0]
        row_group = (lax.broadcasted_iota(jnp.int32, (k, LANES), 0) % LANES) // S5_GROUP
        for g in range(S5_GPB):
            wx_ref[:, g * LANES:(g + 1) * LANES] = jnp.where(row_group == g, w, jnp.zeros_like(w))

    s = _dot(_s5_rows(h_ref, rows), wx_ref[...])
    for g in range(S5_GPB):
        o_ref[:, g, :] = s[:, g * LANES:(g + 1) * LANES]


def _s5_sum(h, w, batch, seq, *, nb=2):
    nc = seq // S5_T
    nblk, k, _ = w.shape
    return pl.pallas_call(
        functools.partial(_s5_sum_kernel, rows=nb * nc),
        grid=(nblk, batch // nb),
        in_specs=[pl.BlockSpec((nb * seq, LANES), lambda j, b: (b, j)),
                  pl.BlockSpec((1, k, LANES), lambda j, b: (j, 0, 0))],
        out_specs=pl.BlockSpec((nb * nc, S5_GPB, LANES), lambda j, b: (b, j, 0)),
        out_shape=jax.ShapeDtypeStruct((batch * nc, S5_GROUPS, LANES), F32),
        scratch_shapes=[pltpu.VMEM((k, S5_GPB * LANES), BF16)],
        compiler_params=_cparams("parallel", "arbitrary"),
        name="s5_chunk_sum",
    )(h, w)


def _s5_scan_kernel(s_ref, a1_ref, a2_ref, o_ref, x_ref, ss_ref, *, tc, nb):
    @pl.when(pl.program_id(0) == 0)
    def _():
        x_ref[...] = jnp.zeros_like(x_ref)

    a1 = a1_ref[...]
    a2 = a2_ref[...]

    rows, lanes = a1.shape
    for b in range(nb):
        blk = s_ref[b].reshape(tc * rows, lanes)
        ss_ref[b] = pltpu.roll(blk, S5_STATE, axis=1).reshape(tc, rows, lanes)

    def run(batches):
        def step(c, carry):
            new = []
            for b, (x, xs) in zip(batches, carry):
                o_ref[b, c] = x
                new.append((a1 * x + a2 * xs + s_ref[b, c], a1 * xs - a2 * x + ss_ref[b, c]))
            return tuple(new)

        init = tuple((x_ref[b], pltpu.roll(x_ref[b], S5_STATE, axis=1)) for b in batches)
        final = lax.fori_loop(0, tc, step, init, unroll=4)
        for b, (x, _) in zip(batches, final):
            x_ref[b] = x

    for b0 in range(0, nb, 2):
        run(range(b0, min(b0 + 2, nb)))


def _s5_scan(s, a1, a2, *, tc=64):
    nb, nc, rows, lanes = s.shape
    return pl.pallas_call(
        functools.partial(_s5_scan_kernel, tc=tc, nb=nb),
        grid=(nc // tc,),
        in_specs=[pl.BlockSpec((nb, tc, rows, lanes), lambda i: (0, i, 0, 0)),
                  pl.BlockSpec((rows, lanes), lambda i: (0, 0)),
                  pl.BlockSpec((rows, lanes), lambda i: (0, 0))],
        out_specs=pl.BlockSpec((nb, tc, rows, lanes), lambda i: (0, i, 0, 0)),
        out_shape=jax.ShapeDtypeStruct(s.shape, F32),
        scratch_shapes=[pltpu.VMEM((nb, rows, lanes), F32), pltpu.VMEM((nb, tc, rows, lanes), F32)],
        compiler_params=_cparams("arbitrary"),
        name="s5_chunk_scan",
    )(s, a1, a2)


def _s5_spread():
    src = np.arange(S5_ROW)
    col = np.arange(S5_T * LANES)
    hit = (src[:, None] // S5_GROUP == col[None, :] // LANES) & (src[:, None] % S5_GROUP == col[None, :] % S5_GROUP)
    return jnp.asarray(hit, dtype=BF16)


def _s5_out_kernel(h_ref, x_ref, lag_ref, wcc_ref, spread_ref, o_ref, wt_ref, wc_ref, *, nc):
    t = S5_T
    nq = 4
    qt = t // nq
    qw = qt * LANES

    @pl.when(pl.program_id(1) == 0)
    def _():
        in_group = (lax.broadcasted_iota(jnp.int32, (LANES, LANES), 0) // S5_GROUP
                    == lax.broadcasted_iota(jnp.int32, (LANES, LANES), 1) // S5_GROUP)
        lags = [jnp.where(in_group, _dot(lag_ref[0, lg], spread_ref[:S5_GROUP, :LANES]), 0.0).astype(BF16)
                for lg in range(t)]
        zero = jnp.zeros((LANES, LANES), BF16)
        for tp in range(t):
            for s in range((tp // qt + 1) * qt):
                blk = lags[tp - s] if tp >= s else zero
                wt_ref[s * LANES:(s + 1) * LANES, tp * LANES:(tp + 1) * LANES] = blk
        kc = wc_ref.shape[0]
        row_group = lax.broadcasted_iota(jnp.int32, (kc, qw), 0) // (2 * S5_STATE)
        col_group = (lax.broadcasted_iota(jnp.int32, (kc, qw), 1) % LANES) // S5_GROUP
        for q in range(nq):
            cols = slice(q * qw, (q + 1) * qw)
            wide = _dot(wcc_ref[0], spread_ref[:, cols])
            wc_ref[:, cols] = jnp.where(row_group == col_group, wide, 0.0).astype(BF16)

    a = _s5_rows(h_ref, nc)
    xp = jnp.concatenate([x_ref[:, g, :] for g in range(S5_GPB)], axis=1).astype(BF16)
    for q in range(nq):
        cols = slice(q * qw, (q + 1) * qw)
        kq = (q + 1) * qw
        y = _dot(a[:, :kq], wt_ref[:kq, cols]) + _dot(xp, wc_ref[:, cols])
        for tt in range(qt):
            o_ref[pl.ds(q * qt + tt, nc, stride=t), :] = y[:, tt * LANES:(tt + 1) * LANES]


def _s5_out(h, xprev, w_lag, w_carry, batch, seq):
    nc = seq // S5_T
    nblk = w_lag.shape[0]
    kc = w_carry.shape[1]
    k = S5_T * LANES
    return pl.pallas_call(
        functools.partial(_s5_out_kernel, nc=nc),
        grid=(nblk, batch),
        in_specs=[pl.BlockSpec((seq, LANES), lambda j, b: (b, j)),
                  pl.BlockSpec((nc, S5_GPB, LANES), lambda j, b: (b, j, 0)),
                  pl.BlockSpec((1,) + w_lag.shape[1:], lambda j, b: (j, 0, 0, 0)),
                  pl.BlockSpec((1, kc, S5_ROW), lambda j, b: (j, 0, 0)),
                  _resident((S5_ROW, k))],
        out_specs=pl.BlockSpec((seq, LANES), lambda j, b: (b, j)),
        out_shape=jax.ShapeDtypeStruct(h.shape, F32),
        scratch_shapes=[pltpu.VMEM((k, k), BF16), pltpu.VMEM((kc, k), BF16)],
        compiler_params=_cparams("parallel", "arbitrary"),
        name="s5_chunk_out",
    )(h, xprev, w_lag, w_carry, _s5_spread())


def _s5_tables(a_re, a_im, log_step, b_re, b_im, c_re, c_im):
    hp = lax.Precision.HIGHEST
    t = S5_T
    step = jnp.exp(log_step)[:, None]
    mag = jnp.exp(step * a_re)
    ab_r = mag * jnp.cos(step * a_im)
    ab_i = mag * jnp.sin(step * a_im)
    den = a_re * a_re + a_im * a_im
    coef_r = ((ab_r - 1.0) * a_re + ab_i * a_im) / den
    coef_i = (ab_i * a_re - (ab_r - 1.0) * a_im) / den
    bb_r = coef_r[..., None] * b_re - coef_i[..., None] * b_im
    bb_i = coef_r[..., None] * b_im + coef_i[..., None] * b_re
    j = jnp.arange(t + 1, dtype=F32)[:, None, None]
    pmag = jnp.exp(j * (step * a_re)[None])
    pw_r = pmag * jnp.cos(j * (step * a_im)[None])
    pw_i = pmag * jnp.sin(j * (step * a_im)[None])

    rev_r, rev_i = pw_r[t - 1::-1], pw_i[t - 1::-1]
    ws_r = rev_r[:, :, :, None] * bb_r[None] - rev_i[:, :, :, None] * bb_i[None]
    ws_i = rev_r[:, :, :, None] * bb_i[None] + rev_i[:, :, :, None] * bb_r[None]
    w_sum = jnp.concatenate([ws_r, ws_i], axis=2)
    w_sum = w_sum.transpose(1, 0, 3, 2).reshape(S5_GROUPS, S5_ROW, 2 * S5_STATE)

    lb_r = pw_r[:t, :, :, None] * bb_r[None] - pw_i[:t, :, :, None] * bb_i[None]
    lb_i = pw_r[:t, :, :, None] * bb_i[None] + pw_i[:t, :, :, None] * bb_r[None]
    kmat = (jnp.einsum('gdp,jgpc->jgdc', c_re, lb_r, precision=hp)
            - jnp.einsum('gdp,jgpc->jgdc', c_im, lb_i, precision=hp))

    q_r, q_i = pw_r[1:], pw_i[1:]
    wc_r = c_re[None] * q_r[:, :, None, :] - c_im[None] * q_i[:, :, None, :]
    wc_i = -(c_re[None] * q_i[:, :, None, :] + c_im[None] * q_r[:, :, None, :])
    w_carry = jnp.concatenate([wc_r, wc_i], axis=3)
    w_carry = w_carry.transpose(1, 3, 0, 2).reshape(S5_GROUPS, 2 * S5_STATE, S5_ROW)

    a1 = jnp.concatenate([pw_r[t], pw_r[t]], axis=1)
    a2 = jnp.concatenate([-pw_i[t], pw_i[t]], axis=1)

    gl = S5_GPB
    nblk = S5_GROUPS // gl
    ws = w_sum.astype(BF16).reshape(nblk, gl, t, S5_GROUP, 2 * S5_STATE).transpose(0, 2, 1, 3, 4)
    ws = ws.reshape(nblk, t * LANES, 2 * S5_STATE)

    lag = kmat.transpose(0, 1, 3, 2).reshape(t, nblk, LANES, S5_GROUP).astype(BF16)
    lag = lag.transpose(1, 0, 2, 3)
    wc = w_carry.astype(BF16).reshape(nblk, gl * 2 * S5_STATE, S5_ROW)
    return ws, lag, wc, a1, a2


def _s5_ssm(h, tables, batch, seq):
    w_sum, w_lag, w_carry, a1, a2 = tables
    nc = seq // S5_T
    s = _s5_sum(h, w_sum, batch, seq)
    s4 = s.reshape(batch, nc, S5_GROUPS, 2 * S5_STATE)
    xprev = _s5_scan(s4, a1, a2).reshape(s.shape)
    return _s5_out(h, xprev, w_lag, w_carry, batch, seq)


def kernel(x, norm_g, ffn_w_in, ffn_w_out, mix_w_in, attn_sinks, hgrn_lb, mix_w_out, s5_a_re, s5_a_im, s5_log_step, s5_b_re, s5_b_im, s5_c_re, s5_c_im, s5_d, s5_w_glu, final_g):
    batch, seq, d = x.shape
    n = batch * seq
    x = x.reshape(n, d)

    w_in16 = ffn_w_in.astype(BF16)
    w_out16 = ffn_w_out.astype(BF16)
    mix_in16 = mix_w_in.astype(BF16)
    mix_out16 = mix_w_out.astype(BF16)
    glu16 = s5_w_glu.astype(BF16)
    fg = final_g.reshape(1, d)

    lb_p = jax.nn.softmax(hgrn_lb.astype(F32), axis=0)
    lb_all = jnp.cumsum(lb_p, axis=0) - lb_p[0]

    for layer in range(DEPTH):
        g3 = norm_g[layer].reshape(3, 1, d)
        if layer % 2 == 0:
            e = layer // 2
            x = _ffn(x, g3[0], w_in16, w_out16, layer, 0, g3[1])
            proj = _norm_proj(x, g3[1], mix_in16[e])
            attn = _attention(proj, attn_sinks[e], batch, seq)
            rec = _hgrn(proj, lb_all[e], batch, seq)
            pre, pre_args = "mix", (attn, rec, mix_out16, e)
        else:
            o = layer // 2
            tables = _s5_tables(s5_a_re[o], s5_a_im[o], s5_log_step[o], s5_b_re[o], s5_b_im[o],
                                s5_c_re[o], s5_c_im[o])
            x, h = _ffn(x, g3[0], w_in16, w_out16, layer, 0, g3[1], post="also_norm")
            y = _s5_ssm(h, tables, batch, seq)
            pre, pre_args = "s5", (h, y, s5_d[o].reshape(1, d), glu16, o)
        last = layer == DEPTH - 1
        x = _ffn(x, g3[2], w_in16, w_out16, layer, 1, fg, pre=pre, pre_args=pre_args,
                 post="norm_only" if last else "none")
    return x.reshape(batch, seq, d)
```

```python
import functools
import math

import jax
import jax.numpy as jnp
import numpy as np
from jax import lax
from jax.experimental import pallas as pl
from jax.experimental.pallas import tpu as pltpu

F32 = jnp.float32
BF16 = jnp.bfloat16

EPS = 1e-6
DEPTH = 4
D_MODEL = 1024
D_FF = 2816
A_HEADS = 8
A_KV_HEADS = 2
A_GROUP = A_HEADS // A_KV_HEADS
A_HEAD_DIM = 64
WINDOW = 128
A_Q_W = A_HEADS * A_HEAD_DIM
A_KV_W = A_KV_HEADS * A_HEAD_DIM
B_HEADS = 4
B_DIM = 128
B_CHUNK = 32
B_W = B_HEADS * B_DIM
MIX_IN = A_Q_W + 2 * A_KV_W + 4 * B_W
S5_GROUP = 16
S5_GROUPS = D_MODEL // S5_GROUP
S5_STATE = 64
S5_T = 16
S5_ROW = S5_T * S5_GROUP

LANES = 128
VMEM_LIMIT = 56 * 1024 * 1024


def _cparams(*sem):
    return pltpu.CompilerParams(dimension_semantics=sem, vmem_limit_bytes=VMEM_LIMIT)


def _sigmoid(x):
    return 1.0 / (1.0 + jnp.exp(-x))


def _silu(x):
    return x * _sigmoid(x)


def _rms(x, g):
    return x * lax.rsqrt(jnp.mean(x * x, axis=-1, keepdims=True) + EPS) * g


def _dot(a, b):
    return jnp.dot(a, b, preferred_element_type=F32)


def _dot_nt(a, b):
    return lax.dot_general(a, b, (((1,), (1,)), ((), ())), preferred_element_type=F32)


def _mix_residual(x_ref, a_ref, b_ref, w_ref):
    ka = a_ref.shape[1]
    return x_ref[...] + _dot(a_ref[...], w_ref[:ka, :]) + _dot(b_ref[...], w_ref[ka:, :])


def _s5_residual(x_ref, h_ref, y_ref, d_ref, w_ref):
    y = y_ref[...] + d_ref[...] * h_ref[...]
    inner = math.sqrt(2.0 / math.pi) * (y + 0.044715 * (y * y * y))
    ge = y * (0.5 * (1.0 + jnp.tanh(inner)))
    z = _dot(ge.astype(BF16), w_ref[...])
    d = x_ref.shape[1]
    return x_ref[...] + z[:, :d] * _sigmoid(z[:, d:])


_FFN_PRE = {"none": (1, lambda x_ref: x_ref[...]), "mix": (4, _mix_residual), "s5": (5, _s5_residual)}


def _ffn_kernel(*refs, tf, pre, post):
    n_pre, pre_fn = _FFN_PRE[pre]
    g_ref, wi_ref, wo_ref, g2_ref = refs[n_pre:n_pre + 4]
    o_refs = refs[n_pre + 4:]
    x = pre_fn(*refs[:n_pre])
    h = _rms(x, g_ref[...]).astype(BF16)
    acc = None
    for c in range(D_FF // tf):
        gate = _dot(h, wi_ref[:, c * tf:(c + 1) * tf])
        up = _dot(h, wi_ref[:, D_FF + c * tf:D_FF + (c + 1) * tf])
        act = (_silu(gate) * up).astype(BF16)
        part = _dot(act, wo_ref[c * tf:(c + 1) * tf, :])
        acc = part if acc is None else acc + part
    y = x + 0.5 * acc
    if post == "norm_only":
        o_refs[0][...] = _rms(y, g2_ref[...])
    else:
        o_refs[0][...] = y
        if post == "also_norm":
            o_refs[1][...] = _rms(y, g2_ref[...])


def _resident(shape, *index):
    lead = len(index)
    block = (None,) * lead + tuple(shape)
    full = tuple(index) + (0,) * len(shape)
    return pl.BlockSpec(block, lambda *_: full, pipeline_mode=pl.Buffered(1))


def _row_tile(stream_bytes_per_row, temp_bytes_per_row, resident_bytes):
    for tm in (1024, 512, 256):
        if resident_bytes + tm * (2 * stream_bytes_per_row + temp_bytes_per_row) <= VMEM_LIMIT:
            return tm
    raise ValueError("row tile does not fit VMEM")


def _ffn(x, g, w_in, w_out, layer, which, g2, *, pre="none", pre_args=(), post="none", tf=256):
    n, d = x.shape
    two = post == "also_norm"
    row_ops = [x] + {"mix": list(pre_args[:2]), "s5": list(pre_args[:2]), "none": []}[pre]
    stream = sum(a.shape[1] * a.dtype.itemsize for a in row_ops) + (2 if two else 1) * d * 4
    resident = sum(math.prod(w.shape[-2:]) * w.dtype.itemsize
                   for w in [w_in, w_out] + ([pre_args[-2]] if pre != "none" else []))
    temp = d * (2 + 4 + 4) + 2 * tf * (4 + 4 + 2) + (2 * d * 4 if pre == "s5" else 0)
    tm = _row_tile(stream, temp, resident)
    assert n % tm == 0 and d == D_MODEL

    rows = lambda a: pl.BlockSpec((tm, a.shape[1]), lambda i: (i, 0))
    tile = pl.BlockSpec((tm, d), lambda i: (i, 0))
    if pre == "mix":
        a, b, w, idx = pre_args
        pre_ops, pre_specs = (a, b, w), [rows(a), rows(b), _resident(w.shape[1:], idx)]
    elif pre == "s5":
        h, y, dskip, w, idx = pre_args
        pre_ops = (h, y, dskip, w)
        pre_specs = [tile, tile, _resident(dskip.shape), _resident(w.shape[1:], idx)]
    else:
        pre_ops, pre_specs = (), []
    shape = jax.ShapeDtypeStruct((n, d), F32)
    return pl.pallas_call(
        functools.partial(_ffn_kernel, tf=tf, pre=pre, post=post),
        grid=(n // tm,),
        in_specs=[tile] + pre_specs + [
            _resident((1, d)),
            _resident((d, 2 * D_FF), layer, which),
            _resident((D_FF, d), layer, which),
            _resident((1, d)),
        ],
        out_specs=[tile, tile] if two else tile,
        out_shape=[shape, shape] if two else shape,
        compiler_params=_cparams("parallel"),
        name="ffn",
    )(x, *pre_ops, g, w_in, w_out, g2)


def _norm_proj_kernel(x_ref, g_ref, w_ref, o_ref):
    h = _rms(x_ref[...], g_ref[...]).astype(BF16)
    res = _dot(h, w_ref[...])
    for cb in range(o_ref.shape[0]):
        o_ref[cb] = res[:, cb * LANES:(cb + 1) * LANES]


def _norm_proj(x, g, w):
    n, d = x.shape
    m = w.shape[1]
    tm = _row_tile((d + m) * 4, d * 2 + m * 4, d * m * w.dtype.itemsize)
    return pl.pallas_call(
        _norm_proj_kernel,
        grid=(n // tm,),
        in_specs=[
            pl.BlockSpec((tm, d), lambda i: (i, 0)),
            _resident((1, d)),
            _resident((d, m)),
        ],
        out_specs=pl.BlockSpec((m // LANES, tm, LANES), lambda i: (0, i, 0)),
        out_shape=jax.ShapeDtypeStruct((m // LANES, n, LANES), F32),
        compiler_params=_cparams("parallel"),
        name="norm_proj",
    )(x, g, w)


def _attn_bias():
    w = WINDOW
    slopes = 2.0 ** (-8.0 * np.arange(1, A_HEADS + 1) / A_HEADS)
    dist = (np.arange(w) + w)[:, None] - np.arange(2 * w)[None, :]
    band = (dist >= 0) & (dist < w)
    has_prev = np.arange(2 * w)[None, :] >= w
    val = -slopes.reshape(A_KV_HEADS, A_GROUP, 1, 1) * dist.astype(np.float64)[None, None]
    out = np.stack([np.where(band & has_prev, val, -np.inf), np.where(band, val, -np.inf)])
    return jnp.asarray(out.reshape(2, A_KV_HEADS, A_GROUP * w, 2 * w), dtype=F32)


def _attn_kernel(sink_ref, bias_ref, q_ref, kc_ref, kp_ref, vc_ref, vp_ref, o_ref, *, nq):
    hk = pl.program_id(1)
    tile = pl.program_id(2)
    w = WINDOW
    dh = A_HEAD_DIM
    qw = A_GROUP * dh

    def widen(prev_ref, cur_ref):
        t = jnp.concatenate([prev_ref[...], cur_ref[...]], axis=0)
        r = pltpu.roll(t, dh, axis=1)
        lane = lax.broadcasted_iota(jnp.int32, t.shape, 1)
        t2 = jnp.where(lane // dh == hk, t, r)
        return jnp.concatenate([t2, t2], axis=1).astype(BF16)

    k4 = widen(kp_ref, kc_ref)
    v4 = widen(vp_ref, vc_ref)

    lane_head = lax.broadcasted_iota(jnp.int32, (w, qw), 1) // dh
    grp = lax.broadcasted_iota(jnp.int32, (A_GROUP * w, 1), 0) // w
    sink = jnp.zeros((A_GROUP * w, 1), F32)
    for g in range(A_GROUP):
        sink = jnp.where(grp == g, sink_ref[0, hk * A_GROUP + g], sink)

    bias_rest = bias_ref[1, 0]
    blocks = range(nq)
    keys = [slice(blk * w, (blk + 2) * w) for blk in blocks]

    scores = []
    for blk in blocks:
        q = jnp.concatenate([q_ref[c, blk * w:(blk + 1) * w, :] for c in range(qw // LANES)], axis=1)
        q = q * (dh ** -0.5)
        q4 = jnp.concatenate(
            [jnp.where(lane_head == g, q, 0.0) for g in range(A_GROUP)], axis=0).astype(BF16)
        bias = jnp.where(tile == 0, bias_ref[0, 0], bias_rest) if blk == 0 else bias_rest
        scores.append(_dot_nt(q4, k4[keys[blk]]) + bias)

    probs, denoms = [], []
    for s in scores:
        m = jnp.maximum(jnp.max(s, axis=-1, keepdims=True), sink)
        p = jnp.exp(s - m)
        denoms.append(jnp.sum(p, axis=-1, keepdims=True) + jnp.exp(sink - m))
        probs.append(p.astype(BF16))

    outs = [_dot(probs[blk], v4[keys[blk]]) / denoms[blk] for blk in blocks]

    for blk in blocks:
        o = jnp.zeros((w, qw), F32)
        for g in range(A_GROUP):
            o = o + jnp.where(lane_head == g, outs[blk][g * w:(g + 1) * w, :], 0.0)
        o_ref[blk * w:(blk + 1) * w, :] = o.astype(o_ref.dtype)


def _attention(proj, sinks, batch, seq, *, nq=16):
    n = proj.shape[1]
    rows = nq * WINDOW
    assert seq % rows == 0 and n == batch * seq
    nt = seq // rows
    qw = A_GROUP * A_HEAD_DIM
    qblk = qw // LANES
    kcol = A_Q_W // LANES
    vcol = (A_Q_W + A_KV_W) // LANES
    cur = lambda b, h, i: b * nt + i
    prev = lambda b, h, i: b * nt * nq + jnp.maximum(i * nq - 1, 0)
    bias = _attn_bias()
    return pl.pallas_call(
        functools.partial(_attn_kernel, nq=nq),
        grid=(batch, A_KV_HEADS, nt),
        in_specs=[
            pl.BlockSpec(memory_space=pltpu.SMEM),
            pl.BlockSpec((2, 1) + bias.shape[2:], lambda b, h, i: (0, h, 0, 0)),
            pl.BlockSpec((qblk, rows, LANES), lambda b, h, i: (h, cur(b, h, i), 0)),
            pl.BlockSpec((None, rows, LANES), lambda b, h, i: (kcol, cur(b, h, i), 0)),
            pl.BlockSpec((None, WINDOW, LANES), lambda b, h, i: (kcol, prev(b, h, i), 0)),
            pl.BlockSpec((None, rows, LANES), lambda b, h, i: (vcol, cur(b, h, i), 0)),
            pl.BlockSpec((None, WINDOW, LANES), lambda b, h, i: (vcol, prev(b, h, i), 0)),
        ],
        out_specs=pl.BlockSpec((rows, qw), lambda b, h, i: (cur(b, h, i), h)),
        out_shape=jax.ShapeDtypeStruct((n, A_Q_W), BF16),
        compiler_params=_cparams("parallel", "parallel", "parallel"),
        name="swa",
    )(sinks.reshape(1, A_HEADS), bias, proj, proj, proj, proj, proj)


def _split3(x):
    hi = x.astype(BF16)
    r1 = x - hi.astype(F32)
    mid = r1.astype(BF16)
    lo = (r1 - mid.astype(F32)).astype(BF16)
    return hi, mid, lo


def _hgrn_kernel(lb_ref, q_ref, f_ref, i_ref, g_ref, o_ref, st_ref, *, ts):
    @pl.when(pl.program_id(2) == 0)
    def _():
        st_ref[...] = jnp.zeros_like(st_ref)

    out, st = _hgrn_head(lb_ref[...], q_ref[...], f_ref[...], i_ref[...], g_ref[...], st_ref[...], ts)
    st_ref[...] = st
    o_ref[...] = out.astype(o_ref.dtype)


def _hgrn_head(lb, q, f_logit, v32, g, st, ts):
    c = B_CHUNK
    nchunk = ts // c
    f = lb + (1.0 - lb) * _sigmoid(f_logit)
    k = 1.0 - f
    logf = jnp.log(f)

    sub = LANES
    r2 = lax.broadcasted_iota(jnp.int32, (sub, sub), 0)
    c2 = lax.broadcasted_iota(jnp.int32, (sub, sub), 1)
    causal = ((r2 // c) == (c2 // c)) & (c2 <= r2)

    tri = causal.astype(BF16)
    parts = _split3(logf)
    cum = jnp.concatenate(
        [sum(_dot(tri, p[s0:s0 + sub]) for p in parts) for s0 in range(0, ts, sub)], axis=0)
    tot_rows = [cum[(ch + 1) * c - 1:(ch + 1) * c, :] for ch in range(nchunk)]
    tot = jnp.concatenate([jnp.broadcast_to(t, (c, B_DIM)) for t in tot_rows], axis=0)
    dec_rows = [jnp.exp(t) for t in tot_rows]

    qd = _silu(q) * jnp.exp(cum)
    kinv = (k * jnp.exp(-cum)).astype(BF16)
    kend = k * jnp.exp(tot - cum)
    v16 = v32.astype(BF16)
    v_t = v32.T.astype(BF16)

    per = sub // c
    chunk_gap = r2 // c - c2 // c
    gap1, gap2, gap3 = chunk_gap == 1, chunk_gap == 2, chunk_gap == 3
    one = jnp.ones((1, B_DIM), F32)

    def by_chunk(rows):
        return jnp.concatenate([jnp.broadcast_to(x, (c, B_DIM)) for x in rows], axis=0)

    groups = range(ts // sub)
    rows = [slice(s * sub, (s + 1) * sub) for s in groups]
    decs = [dec_rows[s * per:(s + 1) * per] for s in groups]

    states = []
    for s in groups:
        d = decs[s]
        states.append(st.astype(BF16))
        to_end = by_chunk([d[1] * d[2] * d[3], d[2] * d[3], d[3], one])
        kv_t = _dot(v_t[:, rows[s]], (kend[rows[s]] * to_end).astype(BF16))
        st = st * (d[0] * d[1] * d[2] * d[3]) + kv_t

    scores = []
    for s in groups:
        d = decs[s]
        between1 = by_chunk([one, one, d[1], d[2]])
        between2 = by_chunk([one, one, one, d[1] * d[2]])
        qs = qd[rows[s]]
        same = _dot_nt(qs.astype(BF16), kinv[rows[s]])
        lhs = jnp.concatenate([qs, qs * between1, qs * between2], axis=0).astype(BF16)
        cross = _dot_nt(lhs, kend[rows[s]].astype(BF16))
        sc = jnp.where(causal, same,
                       jnp.where(gap1, cross[:sub],
                                 jnp.where(gap2, cross[sub:2 * sub],
                                           jnp.where(gap3, cross[2 * sub:], 0.0))))
        scores.append(sc.astype(BF16))

    outs = []
    for s in groups:
        d = decs[s]
        from_start = by_chunk([one, d[0], d[0] * d[1], d[0] * d[1] * d[2]])
        outs.append(_dot(scores[s], v16[rows[s]])
                    + _dot_nt((qd[rows[s]] * from_start).astype(BF16), states[s]))
    o = jnp.concatenate(outs, axis=0)

    o = o * lax.rsqrt(jnp.mean(o * o, axis=-1, keepdims=True) + EPS)
    return o * _silu(g), st


def _hgrn(proj, lb, batch, seq, *, ts=4096):
    n = proj.shape[1]
    assert seq % ts == 0 and n == batch * seq
    nt = seq // ts
    base = (A_Q_W + 2 * A_KV_W) // LANES

    def col(kind):
        return pl.BlockSpec((None, ts, B_DIM), lambda b, h, t: (base + kind * B_HEADS + h, b * nt + t, 0))

    return pl.pallas_call(
        functools.partial(_hgrn_kernel, ts=ts),
        grid=(batch, B_HEADS, nt),
        in_specs=[pl.BlockSpec((1, B_DIM), lambda b, h, t: (0, h)), col(0), col(1), col(2), col(3)],
        out_specs=pl.BlockSpec((ts, B_DIM), lambda b, h, t: (b * nt + t, h)),
        out_shape=jax.ShapeDtypeStruct((n, B_W), BF16),
        scratch_shapes=[pltpu.VMEM((B_DIM, B_DIM), F32)],
        compiler_params=_cparams("parallel", "parallel", "arbitrary"),
        name="hgrn2",
    )(lb.reshape(1, B_W), proj, proj, proj, proj)


def _s5_rows(h_ref, nc):
    return jnp.concatenate(
        [h_ref[pl.ds(t, nc, stride=S5_T), :] for t in range(S5_T)], axis=1).astype(BF16)


S5_GPB = LANES // S5_GROUP


def _s5_sum_kernel(h_ref, w_ref, o_ref, wx_ref, *, rows):
    k = S5_T * LANES

    @pl.when(pl.program_id(1) == 0)
    def _():
        w = w_ref[0]
        row_group = (lax.broadcasted_iota(jnp.int32, (k, LANES), 0) % LANES) // S5_GROUP
        for g in range(S5_GPB):
            wx_ref[:, g * LANES:(g + 1) * LANES] = jnp.where(row_group == g, w, jnp.zeros_like(w))

    s = _dot(_s5_rows(h_ref, rows), wx_ref[...])
    for g in range(S5_GPB):
        o_ref[:, g, :] = s[:, g * LANES:(g + 1) * LANES]


def _s5_sum(h, w, batch, seq, *, nb=2):
    nc = seq // S5_T
    nblk, k, _ = w.shape
    assert batch % nb == 0
    return pl.pallas_call(
        functools.partial(_s5_sum_kernel, rows=nb * nc),
        grid=(nblk, batch // nb),
        in_specs=[pl.BlockSpec((nb * seq, LANES), lambda j, b: (b, j)),
                  pl.BlockSpec((1, k, LANES), lambda j, b: (j, 0, 0))],
        out_specs=pl.BlockSpec((nb * nc, S5_GPB, LANES), lambda j, b: (b, j, 0)),
        out_shape=jax.ShapeDtypeStruct((batch * nc, S5_GROUPS, LANES), F32),
        scratch_shapes=[pltpu.VMEM((k, S5_GPB * LANES), BF16)],
        compiler_params=_cparams("parallel", "arbitrary"),
        name="s5_chunk_sum",
    )(h, w)


def _s5_scan_kernel(s_ref, a1_ref, a2_ref, o_ref, x_ref, ss_ref, *, tc, nb):
    @pl.when(pl.program_id(0) == 0)
    def _():
        x_ref[...] = jnp.zeros_like(x_ref)

    a1 = a1_ref[...]
    a2 = a2_ref[...]

    rows, lanes = a1.shape
    for b in range(nb):
        blk = s_ref[b].reshape(tc * rows, lanes)
        ss_ref[b] = pltpu.roll(blk, S5_STATE, axis=1).reshape(tc, rows, lanes)

    def run(batches):
        def step(c, carry):
            new = []
            for b, (x, xs) in zip(batches, carry):
                o_ref[b, c] = x
                new.append((a1 * x + a2 * xs + s_ref[b, c], a1 * xs - a2 * x + ss_ref[b, c]))
            return tuple(new)

        init = tuple((x_ref[b], pltpu.roll(x_ref[b], S5_STATE, axis=1)) for b in batches)
        final = lax.fori_loop(0, tc, step, init, unroll=4)
        for b, (x, _) in zip(batches, final):
            x_ref[b] = x

    for b0 in range(0, nb, 2):
        run(range(b0, min(b0 + 2, nb)))


def _s5_scan(s, a1, a2, *, tc=64):
    nb, nc, rows, lanes = s.shape
    assert nc % tc == 0
    return pl.pallas_call(
        functools.partial(_s5_scan_kernel, tc=tc, nb=nb),
        grid=(nc // tc,),
        in_specs=[pl.BlockSpec((nb, tc, rows, lanes), lambda i: (0, i, 0, 0)),
                  pl.BlockSpec((rows, lanes), lambda i: (0, 0)),
                  pl.BlockSpec((rows, lanes), lambda i: (0, 0))],
        out_specs=pl.BlockSpec((nb, tc, rows, lanes), lambda i: (0, i, 0, 0)),
        out_shape=jax.ShapeDtypeStruct(s.shape, F32),
        scratch_shapes=[pltpu.VMEM((nb, rows, lanes), F32), pltpu.VMEM((nb, tc, rows, lanes), F32)],
        compiler_params=_cparams("arbitrary"),
        name="s5_chunk_scan",
    )(s, a1, a2)


def _s5_spread():
    src = np.arange(S5_ROW)
    col = np.arange(S5_T * LANES)
    hit = (src[:, None] // S5_GROUP == col[None, :] // LANES) & (src[:, None] % S5_GROUP == col[None, :] % S5_GROUP)
    return jnp.asarray(hit, dtype=BF16)


def _s5_out_kernel(h_ref, x_ref, lag_ref, wcc_ref, spread_ref, o_ref, wt_ref, wc_ref, *, nc):
    t = S5_T
    nq = 4
    qt = t // nq
    qw = qt * LANES

    @pl.when(pl.program_id(1) == 0)
    def _():
        in_group = (lax.broadcasted_iota(jnp.int32, (LANES, LANES), 0) // S5_GROUP
                    == lax.broadcasted_iota(jnp.int32, (LANES, LANES), 1) // S5_GROUP)
        lags = [jnp.where(in_group, _dot(lag_ref[0, lg], spread_ref[:S5_GROUP, :LANES]), 0.0).astype(BF16)
                for lg in range(t)]
        zero = jnp.zeros((LANES, LANES), BF16)
        for tp in range(t):
            for s in range((tp // qt + 1) * qt):
                blk = lags[tp - s] if tp >= s else zero
                wt_ref[s * LANES:(s + 1) * LANES, tp * LANES:(tp + 1) * LANES] = blk
        kc = wc_ref.shape[0]
        row_group = lax.broadcasted_iota(jnp.int32, (kc, qw), 0) // (2 * S5_STATE)
        col_group = (lax.broadcasted_iota(jnp.int32, (kc, qw), 1) % LANES) // S5_GROUP
        for q in range(nq):
            cols = slice(q * qw, (q + 1) * qw)
            wide = _dot(wcc_ref[0], spread_ref[:, cols])
            wc_ref[:, cols] = jnp.where(row_group == col_group, wide, 0.0).astype(BF16)

    a = _s5_rows(h_ref, nc)
    xp = jnp.concatenate([x_ref[:, g, :] for g in range(S5_GPB)], axis=1).astype(BF16)
    for q in range(nq):
        cols = slice(q * qw, (q + 1) * qw)
        kq = (q + 1) * qw
        y = _dot(a[:, :kq], wt_ref[:kq, cols]) + _dot(xp, wc_ref[:, cols])
        for tt in range(qt):
            o_ref[pl.ds(q * qt + tt, nc, stride=t), :] = y[:, tt * LANES:(tt + 1) * LANES]


def _s5_out(h, xprev, w_lag, w_carry, batch, seq):
    nc = seq // S5_T
    nblk = w_lag.shape[0]
    kc = w_carry.shape[1]
    k = S5_T * LANES
    return pl.pallas_call(
        functools.partial(_s5_out_kernel, nc=nc),
        grid=(nblk, batch),
        in_specs=[pl.BlockSpec((seq, LANES), lambda j, b: (b, j)),
                  pl.BlockSpec((nc, S5_GPB, LANES), lambda j, b: (b, j, 0)),
                  pl.BlockSpec((1,) + w_lag.shape[1:], lambda j, b: (j, 0, 0, 0)),
                  pl.BlockSpec((1, kc, S5_ROW), lambda j, b: (j, 0, 0)),
                  _resident((S5_ROW, k))],
        out_specs=pl.BlockSpec((seq, LANES), lambda j, b: (b, j)),
        out_shape=jax.ShapeDtypeStruct(h.shape, F32),
        scratch_shapes=[pltpu.VMEM((k, k), BF16), pltpu.VMEM((kc, k), BF16)],
        compiler_params=_cparams("parallel", "arbitrary"),
        name="s5_chunk_out",
    )(h, xprev, w_lag, w_carry, _s5_spread())


def _s5_tables(a_re, a_im, log_step, b_re, b_im, c_re, c_im):
    hp = lax.Precision.HIGHEST
    t = S5_T
    step = jnp.exp(log_step)[:, None]
    mag = jnp.exp(step * a_re)
    ab_r = mag * jnp.cos(step * a_im)
    ab_i = mag * jnp.sin(step * a_im)
    den = a_re * a_re + a_im * a_im
    coef_r = ((ab_r - 1.0) * a_re + ab_i * a_im) / den
    coef_i = (ab_i * a_re - (ab_r - 1.0) * a_im) / den
    bb_r = coef_r[..., None] * b_re - coef_i[..., None] * b_im
    bb_i = coef_r[..., None] * b_im + coef_i[..., None] * b_re
    j = jnp.arange(t + 1, dtype=F32)[:, None, None]
    pmag = jnp.exp(j * (step * a_re)[None])
    pw_r = pmag * jnp.cos(j * (step * a_im)[None])
    pw_i = pmag * jnp.sin(j * (step * a_im)[None])

    rev_r, rev_i = pw_r[t - 1::-1], pw_i[t - 1::-1]
    ws_r = rev_r[:, :, :, None] * bb_r[None] - rev_i[:, :, :, None] * bb_i[None]
    ws_i = rev_r[:, :, :, None] * bb_i[None] + rev_i[:, :, :, None] * bb_r[None]
    w_sum = jnp.concatenate([ws_r, ws_i], axis=2)
    w_sum = w_sum.transpose(1, 0, 3, 2).reshape(S5_GROUPS, S5_ROW, 2 * S5_STATE)

    lb_r = pw_r[:t, :, :, None] * bb_r[None] - pw_i[:t, :, :, None] * bb_i[None]
    lb_i = pw_r[:t, :, :, None] * bb_i[None] + pw_i[:t, :, :, None] * bb_r[None]
    kmat = (jnp.einsum('gdp,jgpc->jgdc', c_re, lb_r, precision=hp)
            - jnp.einsum('gdp,jgpc->jgdc', c_im, lb_i, precision=hp))

    q_r, q_i = pw_r[1:], pw_i[1:]
    wc_r = c_re[None] * q_r[:, :, None, :] - c_im[None] * q_i[:, :, None, :]
    wc_i = -(c_re[None] * q_i[:, :, None, :] + c_im[None] * q_r[:, :, None, :])
    w_carry = jnp.concatenate([wc_r, wc_i], axis=3)
    w_carry = w_carry.transpose(1, 3, 0, 2).reshape(S5_GROUPS, 2 * S5_STATE, S5_ROW)

    a1 = jnp.concatenate([pw_r[t], pw_r[t]], axis=1)
    a2 = jnp.concatenate([-pw_i[t], pw_i[t]], axis=1)

    gl = S5_GPB
    nblk = S5_GROUPS // gl
    ws = w_sum.astype(BF16).reshape(nblk, gl, t, S5_GROUP, 2 * S5_STATE).transpose(0, 2, 1, 3, 4)
    ws = ws.reshape(nblk, t * LANES, 2 * S5_STATE)

    lag = kmat.transpose(0, 1, 3, 2).reshape(t, nblk, LANES, S5_GROUP).astype(BF16)
    lag = lag.transpose(1, 0, 2, 3)
    wc = w_carry.astype(BF16).reshape(nblk, gl * 2 * S5_STATE, S5_ROW)
    return ws, lag, wc, a1, a2


def _s5_ssm(h, tables, batch, seq):
    w_sum, w_lag, w_carry, a1, a2 = tables
    assert seq % S5_T == 0 and h.shape == (batch * seq, D_MODEL)
    nc = seq // S5_T
    s = _s5_sum(h, w_sum, batch, seq)
    s4 = s.reshape(batch, nc, S5_GROUPS, 2 * S5_STATE)
    xprev = _s5_scan(s4, a1, a2).reshape(s.shape)
    return _s5_out(h, xprev, w_lag, w_carry, batch, seq)


def kernel(x, norm_g, ffn_w_in, ffn_w_out, mix_w_in, attn_sinks, hgrn_lb, mix_w_out, s5_a_re, s5_a_im, s5_log_step, s5_b_re, s5_b_im, s5_c_re, s5_c_im, s5_d, s5_w_glu, final_g):
    batch, seq, d = x.shape
    n = batch * seq
    x = x.reshape(n, d)

    w_in16 = ffn_w_in.astype(BF16)
    w_out16 = ffn_w_out.astype(BF16)
    mix_in16 = mix_w_in.astype(BF16)
    mix_out16 = mix_w_out.astype(BF16)
    glu16 = s5_w_glu.astype(BF16)
    fg = final_g.reshape(1, d)

    lb_p = jax.nn.softmax(hgrn_lb.astype(F32), axis=0)
    lb_all = jnp.cumsum(lb_p, axis=0) - lb_p[0]

    for layer in range(DEPTH):
        g3 = norm_g[layer].reshape(3, 1, d)
        if layer % 2 == 0:
            e = layer // 2
            x = _ffn(x, g3[0], w_in16, w_out16, layer, 0, g3[1])
            proj = _norm_proj(x, g3[1], mix_in16[e])
            attn = _attention(proj, attn_sinks[e], batch, seq)
            rec = _hgrn(proj, lb_all[e], batch, seq)
            pre, pre_args = "mix", (attn, rec, mix_out16, e)
        else:
            o = layer // 2
            tables = _s5_tables(s5_a_re[o], s5_a_im[o], s5_log_step[o], s5_b_re[o], s5_b_im[o],
                                s5_c_re[o], s5_c_im[o])
            x, h = _ffn(x, g3[0], w_in16, w_out16, layer, 0, g3[1], post="also_norm")
            y = _s5_ssm(h, tables, batch, seq)
            pre, pre_args = "s5", (h, y, s5_d[o].reshape(1, d), glu16, o)
        last = layer == DEPTH - 1
        x = _ffn(x, g3[2], w_in16, w_out16, layer, 1, fg, pre=pre, pre_args=pre_args,
                 post="norm_only" if last else "none")
    return x.reshape(batch, seq, d)
```

```python
import functools
import math

import jax
import jax.numpy as jnp
import numpy as np
from jax import lax
from jax.experimental import pallas as pl
from jax.experimental.pallas import tpu as pltpu

F32 = jnp.float32
BF16 = jnp.bfloat16

EPS = 1e-6
DEPTH = 4
D_MODEL = 1024
D_FF = 2816
A_HEADS = 8
A_KV_HEADS = 2
A_GROUP = A_HEADS // A_KV_HEADS
A_HEAD_DIM = 64
WINDOW = 128
A_Q_W = A_HEADS * A_HEAD_DIM
A_KV_W = A_KV_HEADS * A_HEAD_DIM
B_HEADS = 4
B_DIM = 128
B_CHUNK = 32
B_W = B_HEADS * B_DIM
MIX_IN = A_Q_W + 2 * A_KV_W + 4 * B_W
S5_GROUP = 16
S5_GROUPS = D_MODEL // S5_GROUP
S5_STATE = 64
S5_T = 16
S5_ROW = S5_T * S5_GROUP

LANES = 128
VMEM_LIMIT = 56 * 1024 * 1024


def _cparams(*sem):
    return pltpu.CompilerParams(dimension_semantics=sem, vmem_limit_bytes=VMEM_LIMIT)


def _sigmoid(x):
    return 1.0 / (1.0 + jnp.exp(-x))


def _silu(x):
    return x * _sigmoid(x)


def _rms(x, g):
    return x * lax.rsqrt(jnp.mean(x * x, axis=-1, keepdims=True) + EPS) * g


def _dot(a, b):
    return jnp.dot(a, b, preferred_element_type=F32)


def _dot_nt(a, b):
    return lax.dot_general(a, b, (((1,), (1,)), ((), ())), preferred_element_type=F32)


def _mix_residual(x_ref, a_ref, b_ref, w_ref):
    ka = a_ref.shape[1]
    return x_ref[...] + _dot(a_ref[...], w_ref[:ka, :]) + _dot(b_ref[...], w_ref[ka:, :])


def _s5_residual(x_ref, h_ref, y_ref, d_ref, w_ref):
    y = y_ref[...] + d_ref[...] * h_ref[...]
    inner = math.sqrt(2.0 / math.pi) * (y + 0.044715 * (y * y * y))
    ge = y * (0.5 * (1.0 + jnp.tanh(inner)))
    z = _dot(ge.astype(BF16), w_ref[...])
    d = x_ref.shape[1]
    return x_ref[...] + z[:, :d] * _sigmoid(z[:, d:])


_FFN_PRE = {"none": (1, lambda x_ref: x_ref[...]), "mix": (4, _mix_residual), "s5": (5, _s5_residual)}


def _ffn_kernel(*refs, tf, pre, post):
    n_pre, pre_fn = _FFN_PRE[pre]
    g_ref, wi_ref, wo_ref, g2_ref = refs[n_pre:n_pre + 4]
    o_refs = refs[n_pre + 4:]
    x = pre_fn(*refs[:n_pre])
    h = _rms(x, g_ref[...]).astype(BF16)
    acc = None
    for c in range(D_FF // tf):
        gate = _dot(h, wi_ref[:, c * tf:(c + 1) * tf])
        up = _dot(h, wi_ref[:, D_FF + c * tf:D_FF + (c + 1) * tf])
        act = (_silu(gate) * up).astype(BF16)
        part = _dot(act, wo_ref[c * tf:(c + 1) * tf, :])
        acc = part if acc is None else acc + part
    y = x + 0.5 * acc
    if post == "norm_only":
        o_refs[0][...] = _rms(y, g2_ref[...])
    else:
        o_refs[0][...] = y
        if post == "also_norm":
            o_refs[1][...] = _rms(y, g2_ref[...])


def _resident(shape, *index):
    lead = len(index)
    block = (None,) * lead + tuple(shape)
    full = tuple(index) + (0,) * len(shape)
    return pl.BlockSpec(block, lambda *_: full, pipeline_mode=pl.Buffered(1))


def _row_tile(stream_bytes_per_row, temp_bytes_per_row, resident_bytes):
    for tm in (1024, 512, 256):
        if resident_bytes + tm * (2 * stream_bytes_per_row + temp_bytes_per_row) <= VMEM_LIMIT:
            return tm
    raise ValueError("row tile does not fit VMEM")


def _ffn(x, g, w_in, w_out, layer, which, g2, *, pre="none", pre_args=(), post="none", tf=256):
    n, d = x.shape
    two = post == "also_norm"
    row_ops = [x] + {"mix": list(pre_args[:2]), "s5": list(pre_args[:2]), "none": []}[pre]
    stream = sum(a.shape[1] * a.dtype.itemsize for a in row_ops) + (2 if two else 1) * d * 4
    resident = sum(math.prod(w.shape[-2:]) * w.dtype.itemsize
                   for w in [w_in, w_out] + ([pre_args[-2]] if pre != "none" else []))
    temp = d * (2 + 4 + 4) + 2 * tf * (4 + 4 + 2) + (2 * d * 4 if pre == "s5" else 0)
    tm = _row_tile(stream, temp, resident)
    assert n % tm == 0 and d == D_MODEL

    rows = lambda a: pl.BlockSpec((tm, a.shape[1]), lambda i: (i, 0))
    tile = pl.BlockSpec((tm, d), lambda i: (i, 0))
    if pre == "mix":
        a, b, w, idx = pre_args
        pre_ops, pre_specs = (a, b, w), [rows(a), rows(b), _resident(w.shape[1:], idx)]
    elif pre == "s5":
        h, y, dskip, w, idx = pre_args
        pre_ops = (h, y, dskip, w)
        pre_specs = [tile, tile, _resident(dskip.shape), _resident(w.shape[1:], idx)]
    else:
        pre_ops, pre_specs = (), []
    shape = jax.ShapeDtypeStruct((n, d), F32)
    return pl.pallas_call(
        functools.partial(_ffn_kernel, tf=tf, pre=pre, post=post),
        grid=(n // tm,),
        in_specs=[tile] + pre_specs + [
            _resident((1, d)),
            _resident((d, 2 * D_FF), layer, which),
            _resident((D_FF, d), layer, which),
            _resident((1, d)),
        ],
        out_specs=[tile, tile] if two else tile,
        out_shape=[shape, shape] if two else shape,
        compiler_params=_cparams("parallel"),
        name="ffn",
    )(x, *pre_ops, g, w_in, w_out, g2)


def _norm_proj_kernel(x_ref, g_ref, w_ref, o_ref):
    h = _rms(x_ref[...], g_ref[...]).astype(BF16)
    res = _dot(h, w_ref[...])
    for cb in range(o_ref.shape[0]):
        o_ref[cb] = res[:, cb * LANES:(cb + 1) * LANES]


def _norm_proj(x, g, w):
    n, d = x.shape
    m = w.shape[1]
    tm = _row_tile((d + m) * 4, d * 2 + m * 4, d * m * w.dtype.itemsize)
    return pl.pallas_call(
        _norm_proj_kernel,
        grid=(n // tm,),
        in_specs=[
            pl.BlockSpec((tm, d), lambda i: (i, 0)),
            _resident((1, d)),
            _resident((d, m)),
        ],
        out_specs=pl.BlockSpec((m // LANES, tm, LANES), lambda i: (0, i, 0)),
        out_shape=jax.ShapeDtypeStruct((m // LANES, n, LANES), F32),
        compiler_params=_cparams("parallel"),
        name="norm_proj",
    )(x, g, w)


def _attn_bias():
    w = WINDOW
    slopes = 2.0 ** (-8.0 * np.arange(1, A_HEADS + 1) / A_HEADS)
    dist = (np.arange(w) + w)[:, None] - np.arange(2 * w)[None, :]
    band = (dist >= 0) & (dist < w)
    has_prev = np.arange(2 * w)[None, :] >= w
    val = -slopes.reshape(A_KV_HEADS, A_GROUP, 1, 1) * dist.astype(np.float64)[None, None]
    out = np.stack([np.where(band & has_prev, val, -np.inf), np.where(band, val, -np.inf)])
    return jnp.asarray(out.reshape(2, A_KV_HEADS, A_GROUP * w, 2 * w), dtype=F32)


def _attn_kernel(sink_ref, bias_ref, q_ref, kc_ref, kp_ref, vc_ref, vp_ref, o_ref, *, nq):
    hk = pl.program_id(1)
    tile = pl.program_id(2)
    w = WINDOW
    dh = A_HEAD_DIM
    qw = A_GROUP * dh

    def widen(prev_ref, cur_ref):
        t = jnp.concatenate([prev_ref[...], cur_ref[...]], axis=0)
        r = pltpu.roll(t, dh, axis=1)
        lane = lax.broadcasted_iota(jnp.int32, t.shape, 1)
        t2 = jnp.where(lane // dh == hk, t, r)
        return jnp.concatenate([t2, t2], axis=1).astype(BF16)

    k4 = widen(kp_ref, kc_ref)
    v4 = widen(vp_ref, vc_ref)

    lane_head = lax.broadcasted_iota(jnp.int32, (w, qw), 1) // dh
    grp = lax.broadcasted_iota(jnp.int32, (A_GROUP * w, 1), 0) // w
    sink = jnp.zeros((A_GROUP * w, 1), F32)
    for g in range(A_GROUP):
        sink = jnp.where(grp == g, sink_ref[0, hk * A_GROUP + g], sink)

    bias_rest = bias_ref[1, 0]
    blocks = range(nq)
    keys = [slice(blk * w, (blk + 2) * w) for blk in blocks]

    scores = []
    for blk in blocks:
        q = jnp.concatenate([q_ref[c, blk * w:(blk + 1) * w, :] for c in range(qw // LANES)], axis=1)
        q = q * (dh ** -0.5)
        q4 = jnp.concatenate(
            [jnp.where(lane_head == g, q, 0.0) for g in range(A_GROUP)], axis=0).astype(BF16)
        bias = jnp.where(tile == 0, bias_ref[0, 0], bias_rest) if blk == 0 else bias_rest
        scores.append(_dot_nt(q4, k4[keys[blk]]) + bias)

    probs, denoms = [], []
    for s in scores:
        m = jnp.maximum(jnp.max(s, axis=-1, keepdims=True), sink)
        p = jnp.exp(s - m)
        denoms.append(jnp.sum(p, axis=-1, keepdims=True) + jnp.exp(sink - m))
        probs.append(p.astype(BF16))

    outs = [_dot(probs[blk], v4[keys[blk]]) / denoms[blk] for blk in blocks]

    for blk in blocks:
        o = outs[blk][(A_GROUP - 1) * w:, :]
        for g in range(A_GROUP - 2, -1, -1):
            o = jnp.where(lane_head == g, outs[blk][g * w:(g + 1) * w, :], o)
        o_ref[blk * w:(blk + 1) * w, :] = o.astype(o_ref.dtype)


def _attention(proj, sinks, batch, seq, *, nq=16):
    n = proj.shape[1]
    rows = nq * WINDOW
    assert seq % rows == 0 and n == batch * seq
    nt = seq // rows
    qw = A_GROUP * A_HEAD_DIM
    qblk = qw // LANES
    kcol = A_Q_W // LANES
    vcol = (A_Q_W + A_KV_W) // LANES
    cur = lambda b, h, i: b * nt + i
    prev = lambda b, h, i: b * nt * nq + jnp.maximum(i * nq - 1, 0)
    bias = _attn_bias()
    return pl.pallas_call(
        functools.partial(_attn_kernel, nq=nq),
        grid=(batch, A_KV_HEADS, nt),
        in_specs=[
            pl.BlockSpec(memory_space=pltpu.SMEM),
            pl.BlockSpec((2, 1) + bias.shape[2:], lambda b, h, i: (0, h, 0, 0)),
            pl.BlockSpec((qblk, rows, LANES), lambda b, h, i: (h, cur(b, h, i), 0)),
            pl.BlockSpec((None, rows, LANES), lambda b, h, i: (kcol, cur(b, h, i), 0)),
            pl.BlockSpec((None, WINDOW, LANES), lambda b, h, i: (kcol, prev(b, h, i), 0)),
            pl.BlockSpec((None, rows, LANES), lambda b, h, i: (vcol, cur(b, h, i), 0)),
            pl.BlockSpec((None, WINDOW, LANES), lambda b, h, i: (vcol, prev(b, h, i), 0)),
        ],
        out_specs=pl.BlockSpec((rows, qw), lambda b, h, i: (cur(b, h, i), h)),
        out_shape=jax.ShapeDtypeStruct((n, A_Q_W), BF16),
        compiler_params=_cparams("parallel", "parallel", "parallel"),
        name="swa",
    )(sinks.reshape(1, A_HEADS), bias, proj, proj, proj, proj, proj)


def _split3(x):
    hi = x.astype(BF16)
    r1 = x - hi.astype(F32)
    mid = r1.astype(BF16)
    lo = (r1 - mid.astype(F32)).astype(BF16)
    return hi, mid, lo


def _hgrn_kernel(lb_ref, q_ref, f_ref, i_ref, g_ref, o_ref, st_ref, *, ts):
    @pl.when(pl.program_id(2) == 0)
    def _():
        st_ref[...] = jnp.zeros_like(st_ref)

    out, st = _hgrn_head(lb_ref[...], q_ref[...], f_ref[...], i_ref[...], g_ref[...], st_ref[...], ts)
    st_ref[...] = st
    o_ref[...] = out.astype(o_ref.dtype)


def _hgrn_head(lb, q, f_logit, v32, g, st, ts):
    c = B_CHUNK
    nchunk = ts // c
    f = lb + (1.0 - lb) * _sigmoid(f_logit)
    k = 1.0 - f
    logf = jnp.log(f)

    sub = LANES
    r2 = lax.broadcasted_iota(jnp.int32, (sub, sub), 0)
    c2 = lax.broadcasted_iota(jnp.int32, (sub, sub), 1)
    causal = ((r2 // c) == (c2 // c)) & (c2 <= r2)

    tri = causal.astype(BF16)
    parts = _split3(logf)
    cum = jnp.concatenate(
        [sum(_dot(tri, p[s0:s0 + sub]) for p in parts) for s0 in range(0, ts, sub)], axis=0)
    tot_rows = [cum[(ch + 1) * c - 1:(ch + 1) * c, :] for ch in range(nchunk)]
    tot = jnp.concatenate([jnp.broadcast_to(t, (c, B_DIM)) for t in tot_rows], axis=0)
    dec_rows = [jnp.exp(t) for t in tot_rows]

    ecum = jnp.exp(cum)
    qd = _silu(q) * ecum
    kinv = (k / ecum).astype(BF16)
    kend = k * jnp.exp(tot - cum)
    v16 = v32.astype(BF16)
    v_t = v32.T.astype(BF16)

    per = sub // c
    chunk_gap = r2 // c - c2 // c
    gap1, gap2, gap3 = chunk_gap == 1, chunk_gap == 2, chunk_gap == 3
    one = jnp.ones((1, B_DIM), F32)

    def by_chunk(rows):
        return jnp.concatenate([jnp.broadcast_to(x, (c, B_DIM)) for x in rows], axis=0)

    groups = range(ts // sub)
    rows = [slice(s * sub, (s + 1) * sub) for s in groups]
    decs = [dec_rows[s * per:(s + 1) * per] for s in groups]

    states = []
    for s in groups:
        d = decs[s]
        states.append(st.astype(BF16))
        to_end = by_chunk([d[1] * d[2] * d[3], d[2] * d[3], d[3], one])
        kv_t = _dot(v_t[:, rows[s]], (kend[rows[s]] * to_end).astype(BF16))
        st = st * (d[0] * d[1] * d[2] * d[3]) + kv_t

    scores = []
    for s in groups:
        d = decs[s]
        between1 = by_chunk([one, one, d[1], d[2]])
        between2 = by_chunk([one, one, one, d[1] * d[2]])
        qs = qd[rows[s]]
        same = _dot_nt(qs.astype(BF16), kinv[rows[s]])
        lhs = jnp.concatenate([qs, qs * between1, qs * between2], axis=0).astype(BF16)
        cross = _dot_nt(lhs, kend[rows[s]].astype(BF16))
        sc = jnp.where(causal, same,
                       jnp.where(gap1, cross[:sub],
                                 jnp.where(gap2, cross[sub:2 * sub],
                                           jnp.where(gap3, cross[2 * sub:], 0.0))))
        scores.append(sc.astype(BF16))

    outs = []
    for s in groups:
        d = decs[s]
        from_start = by_chunk([one, d[0], d[0] * d[1], d[0] * d[1] * d[2]])
        outs.append(_dot(scores[s], v16[rows[s]])
                    + _dot_nt((qd[rows[s]] * from_start).astype(BF16), states[s]))
    o = jnp.concatenate(outs, axis=0)

    o = o * lax.rsqrt(jnp.mean(o * o, axis=-1, keepdims=True) + EPS)
    return o * _silu(g), st


def _hgrn(proj, lb, batch, seq, *, ts=4096):
    n = proj.shape[1]
    assert seq % ts == 0 and n == batch * seq
    nt = seq // ts
    base = (A_Q_W + 2 * A_KV_W) // LANES

    def col(kind):
        return pl.BlockSpec((None, ts, B_DIM), lambda b, h, t: (base + kind * B_HEADS + h, b * nt + t, 0))

    return pl.pallas_call(
        functools.partial(_hgrn_kernel, ts=ts),
        grid=(batch, B_HEADS, nt),
        in_specs=[pl.BlockSpec((1, B_DIM), lambda b, h, t: (0, h)), col(0), col(1), col(2), col(3)],
        out_specs=pl.BlockSpec((ts, B_DIM), lambda b, h, t: (b * nt + t, h)),
        out_shape=jax.ShapeDtypeStruct((n, B_W), BF16),
        scratch_shapes=[pltpu.VMEM((B_DIM, B_DIM), F32)],
        compiler_params=_cparams("parallel", "parallel", "arbitrary"),
        name="hgrn2",
    )(lb.reshape(1, B_W), proj, proj, proj, proj)


def _s5_rows(h_ref, nc):
    return jnp.concatenate(
        [h_ref[pl.ds(t, nc, stride=S5_T), :] for t in range(S5_T)], axis=1).astype(BF16)


S5_GPB = LANES // S5_GROUP


def _s5_sum_kernel(h_ref, w_ref, o_ref, wx_ref, *, rows):
    k = S5_T * LANES

    @pl.when(pl.program_id(1) == 0)
    def _():
        w = w_ref[0]
        row_group = (lax.broadcasted_iota(jnp.int32, (k, LANES), 0) % LANES) // S5_GROUP
        for g in range(S5_GPB):
            wx_ref[:, g * LANES:(g + 1) * LANES] = jnp.where(row_group == g, w, jnp.zeros_like(w))

    s = _dot(_s5_rows(h_ref, rows), wx_ref[...])
    for g in range(S5_GPB):
        o_ref[:, g, :] = s[:, g * LANES:(g + 1) * LANES]


def _s5_sum(h, w, batch, seq, *, nb=2):
    nc = seq // S5_T
    nblk, k, _ = w.shape
    assert batch % nb == 0
    return pl.pallas_call(
        functools.partial(_s5_sum_kernel, rows=nb * nc),
        grid=(nblk, batch // nb),
        in_specs=[pl.BlockSpec((nb * seq, LANES), lambda j, b: (b, j)),
                  pl.BlockSpec((1, k, LANES), lambda j, b: (j, 0, 0))],
        out_specs=pl.BlockSpec((nb * nc, S5_GPB, LANES), lambda j, b: (b, j, 0)),
        out_shape=jax.ShapeDtypeStruct((batch * nc, S5_GROUPS, LANES), F32),
        scratch_shapes=[pltpu.VMEM((k, S5_GPB * LANES), BF16)],
        compiler_params=_cparams("parallel", "arbitrary"),
        name="s5_chunk_sum",
    )(h, w)


def _s5_scan_kernel(s_ref, a1_ref, a2_ref, o_ref, x_ref, ss_ref, *, tc, nb):
    @pl.when(pl.program_id(0) == 0)
    def _():
        x_ref[...] = jnp.zeros_like(x_ref)

    a1 = a1_ref[...]
    a2 = a2_ref[...]

    rows, lanes = a1.shape
    for b in range(nb):
        blk = s_ref[b].reshape(tc * rows, lanes)
        ss_ref[b] = pltpu.roll(blk, S5_STATE, axis=1).reshape(tc, rows, lanes)

    def run(batches):
        def step(c, carry):
            new = []
            for b, (x, xs) in zip(batches, carry):
                o_ref[b, c] = x
                new.append((a1 * x + a2 * xs + s_ref[b, c], a1 * xs - a2 * x + ss_ref[b, c]))
            return tuple(new)

        init = tuple((x_ref[b], pltpu.roll(x_ref[b], S5_STATE, axis=1)) for b in batches)
        final = lax.fori_loop(0, tc, step, init, unroll=4)
        for b, (x, _) in zip(batches, final):
            x_ref[b] = x

    for b0 in range(0, nb, 2):
        run(range(b0, min(b0 + 2, nb)))


def _s5_scan(s, a1, a2, *, tc=64):
    nb, nc, rows, lanes = s.shape
    assert nc % tc == 0
    return pl.pallas_call(
        functools.partial(_s5_scan_kernel, tc=tc, nb=nb),
        grid=(nc // tc,),
        in_specs=[pl.BlockSpec((nb, tc, rows, lanes), lambda i: (0, i, 0, 0)),
                  pl.BlockSpec((rows, lanes), lambda i: (0, 0)),
                  pl.BlockSpec((rows, lanes), lambda i: (0, 0))],
        out_specs=pl.BlockSpec((nb, tc, rows, lanes), lambda i: (0, i, 0, 0)),
        out_shape=jax.ShapeDtypeStruct(s.shape, F32),
        scratch_shapes=[pltpu.VMEM((nb, rows, lanes), F32), pltpu.VMEM((nb, tc, rows, lanes), F32)],
        compiler_params=_cparams("arbitrary"),
        name="s5_chunk_scan",
    )(s, a1, a2)


def _s5_spread():
    src = np.arange(S5_ROW)
    col = np.arange(S5_T * LANES)
    hit = (src[:, None] // S5_GROUP == col[None, :] // LANES) & (src[:, None] % S5_GROUP == col[None, :] % S5_GROUP)
    return jnp.asarray(hit, dtype=BF16)


def _s5_out_kernel(h_ref, x_ref, lag_ref, wcc_ref, spread_ref, o_ref, wt_ref, wc_ref, *, nc):
    t = S5_T
    nq = 4
    qt = t // nq
    qw = qt * LANES

    @pl.when(pl.program_id(1) == 0)
    def _():
        in_group = (lax.broadcasted_iota(jnp.int32, (LANES, LANES), 0) // S5_GROUP
                    == lax.broadcasted_iota(jnp.int32, (LANES, LANES), 1) // S5_GROUP)
        lags = [jnp.where(in_group, _dot(lag_ref[0, lg], spread_ref[:S5_GROUP, :LANES]), 0.0).astype(BF16)
                for lg in range(t)]
        zero = jnp.zeros((LANES, LANES), BF16)
        for tp in range(t):
            for s in range((tp // qt + 1) * qt):
                blk = lags[tp - s] if tp >= s else zero
                wt_ref[s * LANES:(s + 1) * LANES, tp * LANES:(tp + 1) * LANES] = blk
        kc = wc_ref.shape[0]
        row_group = lax.broadcasted_iota(jnp.int32, (kc, qw), 0) // (2 * S5_STATE)
        col_group = (lax.broadcasted_iota(jnp.int32, (kc, qw), 1) % LANES) // S5_GROUP
        for q in range(nq):
            cols = slice(q * qw, (q + 1) * qw)
            wide = _dot(wcc_ref[0], spread_ref[:, cols])
            wc_ref[:, cols] = jnp.where(row_group == col_group, wide, 0.0).astype(BF16)

    a = _s5_rows(h_ref, nc)
    xp = jnp.concatenate([x_ref[:, g, :] for g in range(S5_GPB)], axis=1).astype(BF16)
    for q in range(nq):
        cols = slice(q * qw, (q + 1) * qw)
        kq = (q + 1) * qw
        y = _dot(a[:, :kq], wt_ref[:kq, cols]) + _dot(xp, wc_ref[:, cols])
        for tt in range(qt):
            o_ref[pl.ds(q * qt + tt, nc, stride=t), :] = y[:, tt * LANES:(tt + 1) * LANES]


def _s5_out(h, xprev, w_lag, w_carry, batch, seq):
    nc = seq // S5_T
    nblk = w_lag.shape[0]
    kc = w_carry.shape[1]
    k = S5_T * LANES
    return pl.pallas_call(
        functools.partial(_s5_out_kernel, nc=nc),
        grid=(nblk, batch),
        in_specs=[pl.BlockSpec((seq, LANES), lambda j, b: (b, j)),
                  pl.BlockSpec((nc, S5_GPB, LANES), lambda j, b: (b, j, 0)),
                  pl.BlockSpec((1,) + w_lag.shape[1:], lambda j, b: (j, 0, 0, 0)),
                  pl.BlockSpec((1, kc, S5_ROW), lambda j, b: (j, 0, 0)),
                  _resident((S5_ROW, k))],
        out_specs=pl.BlockSpec((seq, LANES), lambda j, b: (b, j)),
        out_shape=jax.ShapeDtypeStruct(h.shape, F32),
        scratch_shapes=[pltpu.VMEM((k, k), BF16), pltpu.VMEM((kc, k), BF16)],
        compiler_params=_cparams("parallel", "arbitrary"),
        name="s5_chunk_out",
    )(h, xprev, w_lag, w_carry, _s5_spread())


def _s5_tables(a_re, a_im, log_step, b_re, b_im, c_re, c_im):
    hp = lax.Precision.HIGHEST
    t = S5_T
    step = jnp.exp(log_step)[:, None]
    mag = jnp.exp(step * a_re)
    ab_r = mag * jnp.cos(step * a_im)
    ab_i = mag * jnp.sin(step * a_im)
    den = a_re * a_re + a_im * a_im
    coef_r = ((ab_r - 1.0) * a_re + ab_i * a_im) / den
    coef_i = (ab_i * a_re - (ab_r - 1.0) * a_im) / den
    bb_r = coef_r[..., None] * b_re - coef_i[..., None] * b_im
    bb_i = coef_r[..., None] * b_im + coef_i[..., None] * b_re
    j = jnp.arange(t + 1, dtype=F32)[:, None, None]
    pmag = jnp.exp(j * (step * a_re)[None])
    pw_r = pmag * jnp.cos(j * (step * a_im)[None])
    pw_i = pmag * jnp.sin(j * (step * a_im)[None])

    rev_r, rev_i = pw_r[t - 1::-1], pw_i[t - 1::-1]
    ws_r = rev_r[:, :, :, None] * bb_r[None] - rev_i[:, :, :, None] * bb_i[None]
    ws_i = rev_r[:, :, :, None] * bb_i[None] + rev_i[:, :, :, None] * bb_r[None]
    w_sum = jnp.concatenate([ws_r, ws_i], axis=2)
    w_sum = w_sum.transpose(1, 0, 3, 2).reshape(S5_GROUPS, S5_ROW, 2 * S5_STATE)

    lb_r = pw_r[:t, :, :, None] * bb_r[None] - pw_i[:t, :, :, None] * bb_i[None]
    lb_i = pw_r[:t, :, :, None] * bb_i[None] + pw_i[:t, :, :, None] * bb_r[None]
    kmat = (jnp.einsum('gdp,jgpc->jgdc', c_re, lb_r, precision=hp)
            - jnp.einsum('gdp,jgpc->jgdc', c_im, lb_i, precision=hp))

    q_r, q_i = pw_r[1:], pw_i[1:]
    wc_r = c_re[None] * q_r[:, :, None, :] - c_im[None] * q_i[:, :, None, :]
    wc_i = -(c_re[None] * q_i[:, :, None, :] + c_im[None] * q_r[:, :, None, :])
    w_carry = jnp.concatenate([wc_r, wc_i], axis=3)
    w_carry = w_carry.transpose(1, 3, 0, 2).reshape(S5_GROUPS, 2 * S5_STATE, S5_ROW)

    a1 = jnp.concatenate([pw_r[t], pw_r[t]], axis=1)
    a2 = jnp.concatenate([-pw_i[t], pw_i[t]], axis=1)

    gl = S5_GPB
    nblk = S5_GROUPS // gl
    ws = w_sum.astype(BF16).reshape(nblk, gl, t, S5_GROUP, 2 * S5_STATE).transpose(0, 2, 1, 3, 4)
    ws = ws.reshape(nblk, t * LANES, 2 * S5_STATE)

    lag = kmat.transpose(0, 1, 3, 2).reshape(t, nblk, LANES, S5_GROUP).astype(BF16)
    lag = lag.transpose(1, 0, 2, 3)
    wc = w_carry.astype(BF16).reshape(nblk, gl * 2 * S5_STATE, S5_ROW)
    return ws, lag, wc, a1, a2


def _s5_ssm(h, tables, batch, seq):
    w_sum, w_lag, w_carry, a1, a2 = tables
    assert seq % S5_T == 0 and h.shape == (batch * seq, D_MODEL)
    nc = seq // S5_T
    s = _s5_sum(h, w_sum, batch, seq)
    s4 = s.reshape(batch, nc, S5_GROUPS, 2 * S5_STATE)
    xprev = _s5_scan(s4, a1, a2).reshape(s.shape)
    return _s5_out(h, xprev, w_lag, w_carry, batch, seq)


def kernel(x, norm_g, ffn_w_in, ffn_w_out, mix_w_in, attn_sinks, hgrn_lb, mix_w_out, s5_a_re, s5_a_im, s5_log_step, s5_b_re, s5_b_im, s5_c_re, s5_c_im, s5_d, s5_w_glu, final_g):
    batch, seq, d = x.shape
    n = batch * seq
    x = x.reshape(n, d)

    w_in16 = ffn_w_in.astype(BF16)
    w_out16 = ffn_w_out.astype(BF16)
    mix_in16 = mix_w_in.astype(BF16)
    mix_out16 = mix_w_out.astype(BF16)
    glu16 = s5_w_glu.astype(BF16)
    fg = final_g.reshape(1, d)

    lb_p = jax.nn.softmax(hgrn_lb.astype(F32), axis=0)
    lb_all = jnp.cumsum(lb_p, axis=0) - lb_p[0]

    for layer in range(DEPTH):
        g3 = norm_g[layer].reshape(3, 1, d)
        if layer % 2 == 0:
            e = layer // 2
            x = _ffn(x, g3[0], w_in16, w_out16, layer, 0, g3[1])
            proj = _norm_proj(x, g3[1], mix_in16[e])
            attn = _attention(proj, attn_sinks[e], batch, seq)
            rec = _hgrn(proj, lb_all[e], batch, seq)
            pre, pre_args = "mix", (attn, rec, mix_out16, e)
        else:
            o = layer // 2
            tables = _s5_tables(s5_a_re[o], s5_a_im[o], s5_log_step[o], s5_b_re[o], s5_b_im[o],
                                s5_c_re[o], s5_c_im[o])
            x, h = _ffn(x, g3[0], w_in16, w_out16, layer, 0, g3[1], post="also_norm")
            y = _s5_ssm(h, tables, batch, seq)
            pre, pre_args = "s5", (h, y, s5_d[o].reshape(1, d), glu16, o)
        last = layer == DEPTH - 1
        x = _ffn(x, g3[2], w_in16, w_out16, layer, 1, fg, pre=pre, pre_args=pre_args,
                 post="norm_only" if last else "none")
    return x.reshape(batch, seq, d)
```

```python
import functools
import math

import jax
import jax.numpy as jnp
import numpy as np
from jax import lax
from jax.experimental import pallas as pl
from jax.experimental.pallas import tpu as pltpu

F32 = jnp.float32
BF16 = jnp.bfloat16

EPS = 1e-6
DEPTH = 4
D_MODEL = 1024
D_FF = 2816
A_HEADS = 8
A_KV_HEADS = 2
A_GROUP = A_HEADS // A_KV_HEADS
A_HEAD_DIM = 64
WINDOW = 128
A_Q_W = A_HEADS * A_HEAD_DIM
A_KV_W = A_KV_HEADS * A_HEAD_DIM
B_HEADS = 4
B_DIM = 128
B_CHUNK = 32
B_W = B_HEADS * B_DIM
MIX_IN = A_Q_W + 2 * A_KV_W + 4 * B_W
S5_GROUP = 16
S5_GROUPS = D_MODEL // S5_GROUP
S5_STATE = 64
S5_T = 16
S5_ROW = S5_T * S5_GROUP

LANES = 128
VMEM_LIMIT = 56 * 1024 * 1024


def _cparams(*sem):
    return pltpu.CompilerParams(dimension_semantics=sem, vmem_limit_bytes=VMEM_LIMIT)


def _sigmoid(x):
    return 1.0 / (1.0 + jnp.exp(-x))


def _silu(x):
    return x * _sigmoid(x)


def _rms(x, g):
    return x * lax.rsqrt(jnp.mean(x * x, axis=-1, keepdims=True) + EPS) * g


def _dot(a, b):
    return jnp.dot(a, b, preferred_element_type=F32)


def _dot_nt(a, b):
    return lax.dot_general(a, b, (((1,), (1,)), ((), ())), preferred_element_type=F32)


def _mix_residual(x_ref, a_ref, b_ref, w_ref):
    ka = a_ref.shape[1]
    return x_ref[...] + _dot(a_ref[...], w_ref[:ka, :]) + _dot(b_ref[...], w_ref[ka:, :])


def _s5_residual(x_ref, h_ref, y_ref, d_ref, w_ref):
    y = y_ref[...] + d_ref[...] * h_ref[...]
    inner = math.sqrt(2.0 / math.pi) * (y + 0.044715 * (y * y * y))
    ge = y * (0.5 * (1.0 + jnp.tanh(inner)))
    z = _dot(ge.astype(BF16), w_ref[...])
    d = x_ref.shape[1]
    return x_ref[...] + z[:, :d] * _sigmoid(z[:, d:])


_FFN_PRE = {"none": (1, lambda x_ref: x_ref[...]), "mix": (4, _mix_residual), "s5": (5, _s5_residual)}


def _ffn_kernel(*refs, tf, pre, post):
    n_pre, pre_fn = _FFN_PRE[pre]
    g_ref, wi_ref, wo_ref, g2_ref = refs[n_pre:n_pre + 4]
    o_refs = refs[n_pre + 4:]
    x = pre_fn(*refs[:n_pre])
    h = _rms(x, g_ref[...]).astype(BF16)
    acc = None
    for c in range(D_FF // tf):
        gate = _dot(h, wi_ref[:, c * tf:(c + 1) * tf])
        up = _dot(h, wi_ref[:, D_FF + c * tf:D_FF + (c + 1) * tf])
        act = (_silu(gate) * up).astype(BF16)
        part = _dot(act, wo_ref[c * tf:(c + 1) * tf, :])
        acc = part if acc is None else acc + part
    y = x + 0.5 * acc
    if post == "norm_only":
        o_refs[0][...] = _rms(y, g2_ref[...])
    else:
        o_refs[0][...] = y
        if post == "also_norm":
            o_refs[1][...] = _rms(y, g2_ref[...])


def _resident(shape, *index):
    lead = len(index)
    block = (None,) * lead + tuple(shape)
    full = tuple(index) + (0,) * len(shape)
    return pl.BlockSpec(block, lambda *_: full, pipeline_mode=pl.Buffered(1))


def _row_tile(stream_bytes_per_row, temp_bytes_per_row, resident_bytes):
    for tm in (1024, 512, 256):
        if resident_bytes + tm * (2 * stream_bytes_per_row + temp_bytes_per_row) <= VMEM_LIMIT:
            return tm
    raise ValueError("row tile does not fit VMEM")


def _ffn(x, g, w_in, w_out, layer, which, g2, *, pre="none", pre_args=(), post="none", tf=256):
    n, d = x.shape
    two = post == "also_norm"
    row_ops = [x] + {"mix": list(pre_args[:2]), "s5": list(pre_args[:2]), "none": []}[pre]
    stream = sum(a.shape[1] * a.dtype.itemsize for a in row_ops) + (2 if two else 1) * d * 4
    resident = sum(math.prod(w.shape[-2:]) * w.dtype.itemsize
                   for w in [w_in, w_out] + ([pre_args[-2]] if pre != "none" else []))
    temp = d * (2 + 4 + 4) + 2 * tf * (4 + 4 + 2) + (2 * d * 4 if pre == "s5" else 0)
    tm = _row_tile(stream, temp, resident)
    assert n % tm == 0 and d == D_MODEL

    rows = lambda a: pl.BlockSpec((tm, a.shape[1]), lambda i: (i, 0))
    tile = pl.BlockSpec((tm, d), lambda i: (i, 0))
    if pre == "mix":
        a, b, w, idx = pre_args
        pre_ops, pre_specs = (a, b, w), [rows(a), rows(b), _resident(w.shape[1:], idx)]
    elif pre == "s5":
        h, y, dskip, w, idx = pre_args
        pre_ops = (h, y, dskip, w)
        pre_specs = [tile, tile, _resident(dskip.shape), _resident(w.shape[1:], idx)]
    else:
        pre_ops, pre_specs = (), []
    shape = jax.ShapeDtypeStruct((n, d), F32)
    return pl.pallas_call(
        functools.partial(_ffn_kernel, tf=tf, pre=pre, post=post),
        grid=(n // tm,),
        in_specs=[tile] + pre_specs + [
            _resident((1, d)),
            _resident((d, 2 * D_FF), layer, which),
            _resident((D_FF, d), layer, which),
            _resident((1, d)),
        ],
        out_specs=[tile, tile] if two else tile,
        out_shape=[shape, shape] if two else shape,
        compiler_params=_cparams("parallel"),
        name="ffn",
    )(x, *pre_ops, g, w_in, w_out, g2)


def _norm_proj_kernel(x_ref, g_ref, w_ref, o_ref):
    h = _rms(x_ref[...], g_ref[...]).astype(BF16)
    res = _dot(h, w_ref[...])
    for cb in range(o_ref.shape[0]):
        o_ref[cb] = res[:, cb * LANES:(cb + 1) * LANES]


def _norm_proj(x, g, w):
    n, d = x.shape
    m = w.shape[1]
    tm = _row_tile((d + m) * 4, d * 2 + m * 4, d * m * w.dtype.itemsize)
    return pl.pallas_call(
        _norm_proj_kernel,
        grid=(n // tm,),
        in_specs=[
            pl.BlockSpec((tm, d), lambda i: (i, 0)),
            _resident((1, d)),
            _resident((d, m)),
        ],
        out_specs=pl.BlockSpec((m // LANES, tm, LANES), lambda i: (0, i, 0)),
        out_shape=jax.ShapeDtypeStruct((m // LANES, n, LANES), F32),
        compiler_params=_cparams("parallel"),
        name="norm_proj",
    )(x, g, w)


def _attn_bias():
    w = WINDOW
    slopes = 2.0 ** (-8.0 * np.arange(1, A_HEADS + 1) / A_HEADS)
    dist = (np.arange(w) + w)[:, None] - np.arange(2 * w)[None, :]
    band = (dist >= 0) & (dist < w)
    has_prev = np.arange(2 * w)[None, :] >= w
    val = -slopes.reshape(A_KV_HEADS, A_GROUP, 1, 1) * dist.astype(np.float64)[None, None]
    out = np.stack([np.where(band & has_prev, val, -np.inf), np.where(band, val, -np.inf)])
    return jnp.asarray(out.reshape(2, A_KV_HEADS, A_GROUP * w, 2 * w), dtype=F32)


def _attn_kernel(sink_ref, bias_ref, q_ref, kc_ref, kp_ref, vc_ref, vp_ref, o_ref, *, nq):
    hk = pl.program_id(1)
    tile = pl.program_id(2)
    w = WINDOW
    dh = A_HEAD_DIM
    qw = A_GROUP * dh

    def widen(prev_ref, cur_ref):
        t = jnp.concatenate([prev_ref[...], cur_ref[...]], axis=0)
        r = pltpu.roll(t, dh, axis=1)
        lane = lax.broadcasted_iota(jnp.int32, t.shape, 1)
        t2 = jnp.where(lane // dh == hk, t, r)
        return jnp.concatenate([t2, t2], axis=1).astype(BF16)

    k4 = widen(kp_ref, kc_ref)
    v4 = widen(vp_ref, vc_ref)

    lane_head = lax.broadcasted_iota(jnp.int32, (w, qw), 1) // dh
    grp = lax.broadcasted_iota(jnp.int32, (A_GROUP * w, 1), 0) // w
    sink = jnp.zeros((A_GROUP * w, 1), F32)
    for g in range(A_GROUP):
        sink = jnp.where(grp == g, sink_ref[0, hk * A_GROUP + g], sink)

    bias_rest = bias_ref[1, 0]
    blocks = range(nq)
    keys = [slice(blk * w, (blk + 2) * w) for blk in blocks]

    scores = []
    for blk in blocks:
        q = jnp.concatenate([q_ref[c, blk * w:(blk + 1) * w, :] for c in range(qw // LANES)], axis=1)
        q = q * (dh ** -0.5)
        q4 = jnp.concatenate(
            [jnp.where(lane_head == g, q, 0.0) for g in range(A_GROUP)], axis=0).astype(BF16)
        bias = jnp.where(tile == 0, bias_ref[0, 0], bias_rest) if blk == 0 else bias_rest
        scores.append(_dot_nt(q4, k4[keys[blk]]) + bias)

    probs, denoms = [], []
    for s in scores:
        m = jnp.maximum(jnp.max(s, axis=-1, keepdims=True), sink)
        p = jnp.exp(s - m)
        denoms.append(jnp.sum(p, axis=-1, keepdims=True) + jnp.exp(sink - m))
        probs.append(p.astype(BF16))

    outs = [_dot(probs[blk], v4[keys[blk]]) / denoms[blk] for blk in blocks]

    for blk in blocks:
        o = outs[blk][(A_GROUP - 1) * w:, :]
        for g in range(A_GROUP - 2, -1, -1):
            o = jnp.where(lane_head == g, outs[blk][g * w:(g + 1) * w, :], o)
        o_ref[blk * w:(blk + 1) * w, :] = o.astype(o_ref.dtype)


def _attention(proj, sinks, batch, seq, *, nq=16):
    n = proj.shape[1]
    rows = nq * WINDOW
    assert seq % rows == 0 and n == batch * seq
    nt = seq // rows
    qw = A_GROUP * A_HEAD_DIM
    qblk = qw // LANES
    kcol = A_Q_W // LANES
    vcol = (A_Q_W + A_KV_W) // LANES
    cur = lambda b, h, i: b * nt + i
    prev = lambda b, h, i: b * nt * nq + jnp.maximum(i * nq - 1, 0)
    bias = _attn_bias()
    return pl.pallas_call(
        functools.partial(_attn_kernel, nq=nq),
        grid=(batch, A_KV_HEADS, nt),
        in_specs=[
            pl.BlockSpec(memory_space=pltpu.SMEM),
            pl.BlockSpec((2, 1) + bias.shape[2:], lambda b, h, i: (0, h, 0, 0)),
            pl.BlockSpec((qblk, rows, LANES), lambda b, h, i: (h, cur(b, h, i), 0)),
            pl.BlockSpec((None, rows, LANES), lambda b, h, i: (kcol, cur(b, h, i), 0)),
            pl.BlockSpec((None, WINDOW, LANES), lambda b, h, i: (kcol, prev(b, h, i), 0)),
            pl.BlockSpec((None, rows, LANES), lambda b, h, i: (vcol, cur(b, h, i), 0)),
            pl.BlockSpec((None, WINDOW, LANES), lambda b, h, i: (vcol, prev(b, h, i), 0)),
        ],
        out_specs=pl.BlockSpec((rows, qw), lambda b, h, i: (cur(b, h, i), h)),
        out_shape=jax.ShapeDtypeStruct((n, A_Q_W), BF16),
        compiler_params=_cparams("parallel", "parallel", "parallel"),
        name="swa",
    )(sinks.reshape(1, A_HEADS), bias, proj, proj, proj, proj, proj)


def _split2(x):
    hi = x.astype(BF16)
    lo = (x - hi.astype(F32)).astype(BF16)
    return hi, lo


def _hgrn_kernel(lb_ref, q_ref, f_ref, i_ref, g_ref, o_ref, st_ref, *, ts):
    @pl.when(pl.program_id(2) == 0)
    def _():
        st_ref[...] = jnp.zeros_like(st_ref)

    out, st = _hgrn_head(lb_ref[...], q_ref[...], f_ref[...], i_ref[...], g_ref[...], st_ref[...], ts)
    st_ref[...] = st
    o_ref[...] = out.astype(o_ref.dtype)


def _hgrn_head(lb, q, f_logit, v32, g, st, ts):
    c = B_CHUNK
    nchunk = ts // c
    f = lb + (1.0 - lb) * _sigmoid(f_logit)
    k = 1.0 - f
    logf = jnp.log(f)

    sub = LANES
    r2 = lax.broadcasted_iota(jnp.int32, (sub, sub), 0)
    c2 = lax.broadcasted_iota(jnp.int32, (sub, sub), 1)
    causal = ((r2 // c) == (c2 // c)) & (c2 <= r2)

    tri = causal.astype(BF16)
    parts = _split2(logf)
    cum = jnp.concatenate(
        [sum(_dot(tri, p[s0:s0 + sub]) for p in parts) for s0 in range(0, ts, sub)], axis=0)
    tot_rows = [cum[(ch + 1) * c - 1:(ch + 1) * c, :] for ch in range(nchunk)]
    tot = jnp.concatenate([jnp.broadcast_to(t, (c, B_DIM)) for t in tot_rows], axis=0)
    dec_rows = [jnp.exp(t) for t in tot_rows]

    ecum = jnp.exp(cum)
    qd = _silu(q) * ecum
    kinv = (k / ecum).astype(BF16)
    kend = k * jnp.exp(tot - cum)
    v16 = v32.astype(BF16)
    v_t = v32.T.astype(BF16)

    per = sub // c
    chunk_gap = r2 // c - c2 // c
    gap1, gap2, gap3 = chunk_gap == 1, chunk_gap == 2, chunk_gap == 3
    one = jnp.ones((1, B_DIM), F32)

    def by_chunk(rows):
        return jnp.concatenate([jnp.broadcast_to(x, (c, B_DIM)) for x in rows], axis=0)

    groups = range(ts // sub)
    rows = [slice(s * sub, (s + 1) * sub) for s in groups]
    decs = [dec_rows[s * per:(s + 1) * per] for s in groups]

    states = []
    for s in groups:
        d = decs[s]
        states.append(st.astype(BF16))
        to_end = by_chunk([d[1] * d[2] * d[3], d[2] * d[3], d[3], one])
        kv_t = _dot(v_t[:, rows[s]], (kend[rows[s]] * to_end).astype(BF16))
        st = st * (d[0] * d[1] * d[2] * d[3]) + kv_t

    scores = []
    for s in groups:
        d = decs[s]
        between1 = by_chunk([one, one, d[1], d[2]])
        between2 = by_chunk([one, one, one, d[1] * d[2]])
        qs = qd[rows[s]]
        same = _dot_nt(qs.astype(BF16), kinv[rows[s]])
        lhs = jnp.concatenate([qs, qs * between1, qs * between2], axis=0).astype(BF16)
        cross = _dot_nt(lhs, kend[rows[s]].astype(BF16))
        sc = jnp.where(causal, same,
                       jnp.where(gap1, cross[:sub],
                                 jnp.where(gap2, cross[sub:2 * sub],
                                           jnp.where(gap3, cross[2 * sub:], 0.0))))
        scores.append(sc.astype(BF16))

    outs = []
    for s in groups:
        d = decs[s]
        from_start = by_chunk([one, d[0], d[0] * d[1], d[0] * d[1] * d[2]])
        outs.append(_dot(scores[s], v16[rows[s]])
                    + _dot_nt((qd[rows[s]] * from_start).astype(BF16), states[s]))
    o = jnp.concatenate(outs, axis=0)

    o = o * lax.rsqrt(jnp.mean(o * o, axis=-1, keepdims=True) + EPS)
    return o * _silu(g), st


def _hgrn(proj, lb, batch, seq, *, ts=4096):
    n = proj.shape[1]
    assert seq % ts == 0 and n == batch * seq
    nt = seq // ts
    base = (A_Q_W + 2 * A_KV_W) // LANES

    def col(kind):
        return pl.BlockSpec((None, ts, B_DIM), lambda b, h, t: (base + kind * B_HEADS + h, b * nt + t, 0))

    return pl.pallas_call(
        functools.partial(_hgrn_kernel, ts=ts),
        grid=(batch, B_HEADS, nt),
        in_specs=[pl.BlockSpec((1, B_DIM), lambda b, h, t: (0, h)), col(0), col(1), col(2), col(3)],
        out_specs=pl.BlockSpec((ts, B_DIM), lambda b, h, t: (b * nt + t, h)),
        out_shape=jax.ShapeDtypeStruct((n, B_W), BF16),
        scratch_shapes=[pltpu.VMEM((B_DIM, B_DIM), F32)],
        compiler_params=_cparams("parallel", "parallel", "arbitrary"),
        name="hgrn2",
    )(lb.reshape(1, B_W), proj, proj, proj, proj)


def _s5_rows(h_ref, nc):
    return jnp.concatenate(
        [h_ref[pl.ds(t, nc, stride=S5_T), :] for t in range(S5_T)], axis=1).astype(BF16)


S5_GPB = LANES // S5_GROUP


def _s5_sum_kernel(h_ref, w_ref, o_ref, wx_ref, *, rows):
    k = S5_T * LANES

    @pl.when(pl.program_id(1) == 0)
    def _():
        w = w_ref[0]
        row_group = (lax.broadcasted_iota(jnp.int32, (k, LANES), 0) % LANES) // S5_GROUP
        for g in range(S5_GPB):
            wx_ref[:, g * LANES:(g + 1) * LANES] = jnp.where(row_group == g, w, jnp.zeros_like(w))

    s = _dot(_s5_rows(h_ref, rows), wx_ref[...])
    for g in range(S5_GPB):
        o_ref[:, g, :] = s[:, g * LANES:(g + 1) * LANES]


def _s5_sum(h, w, batch, seq, *, nb=2):
    nc = seq // S5_T
    nblk, k, _ = w.shape
    assert batch % nb == 0
    return pl.pallas_call(
        functools.partial(_s5_sum_kernel, rows=nb * nc),
        grid=(nblk, batch // nb),
        in_specs=[pl.BlockSpec((nb * seq, LANES), lambda j, b: (b, j)),
                  pl.BlockSpec((1, k, LANES), lambda j, b: (j, 0, 0))],
        out_specs=pl.BlockSpec((nb * nc, S5_GPB, LANES), lambda j, b: (b, j, 0)),
        out_shape=jax.ShapeDtypeStruct((batch * nc, S5_GROUPS, LANES), F32),
        scratch_shapes=[pltpu.VMEM((k, S5_GPB * LANES), BF16)],
        compiler_params=_cparams("parallel", "arbitrary"),
        name="s5_chunk_sum",
    )(h, w)


def _s5_scan_kernel(s_ref, a1_ref, a2_ref, o_ref, x_ref, ss_ref, *, tc, nb):
    @pl.when(pl.program_id(0) == 0)
    def _():
        x_ref[...] = jnp.zeros_like(x_ref)

    a1 = a1_ref[...]
    a2 = a2_ref[...]

    rows, lanes = a1.shape
    for b in range(nb):
        blk = s_ref[b].reshape(tc * rows, lanes)
        ss_ref[b] = pltpu.roll(blk, S5_STATE, axis=1).reshape(tc, rows, lanes)

    def run(batches):
        def step(c, carry):
            new = []
            for b, (x, xs) in zip(batches, carry):
                o_ref[b, c] = x
                new.append((a1 * x + a2 * xs + s_ref[b, c], a1 * xs - a2 * x + ss_ref[b, c]))
            return tuple(new)

        init = tuple((x_ref[b], pltpu.roll(x_ref[b], S5_STATE, axis=1)) for b in batches)
        final = lax.fori_loop(0, tc, step, init, unroll=4)
        for b, (x, _) in zip(batches, final):
            x_ref[b] = x

    for b0 in range(0, nb, 2):
        run(range(b0, min(b0 + 2, nb)))


def _s5_scan(s, a1, a2, *, tc=64):
    nb, nc, rows, lanes = s.shape
    assert nc % tc == 0
    return pl.pallas_call(
        functools.partial(_s5_scan_kernel, tc=tc, nb=nb),
        grid=(nc // tc,),
        in_specs=[pl.BlockSpec((nb, tc, rows, lanes), lambda i: (0, i, 0, 0)),
                  pl.BlockSpec((rows, lanes), lambda i: (0, 0)),
                  pl.BlockSpec((rows, lanes), lambda i: (0, 0))],
        out_specs=pl.BlockSpec((nb, tc, rows, lanes), lambda i: (0, i, 0, 0)),
        out_shape=jax.ShapeDtypeStruct(s.shape, F32),
        scratch_shapes=[pltpu.VMEM((nb, rows, lanes), F32), pltpu.VMEM((nb, tc, rows, lanes), F32)],
        compiler_params=_cparams("arbitrary"),
        name="s5_chunk_scan",
    )(s, a1, a2)


def _s5_spread():
    src = np.arange(S5_ROW)
    col = np.arange(S5_T * LANES)
    hit = (src[:, None] // S5_GROUP == col[None, :] // LANES) & (src[:, None] % S5_GROUP == col[None, :] % S5_GROUP)
    return jnp.asarray(hit, dtype=BF16)


def _s5_out_kernel(h_ref, x_ref, lag_ref, wcc_ref, spread_ref, o_ref, wt_ref, wc_ref, *, nc):
    t = S5_T
    nq = 4
    qt = t // nq
    qw = qt * LANES

    @pl.when(pl.program_id(1) == 0)
    def _():
        in_group = (lax.broadcasted_iota(jnp.int32, (LANES, LANES), 0) // S5_GROUP
                    == lax.broadcasted_iota(jnp.int32, (LANES, LANES), 1) // S5_GROUP)
        lags = [jnp.where(in_group, _dot(lag_ref[0, lg], spread_ref[:S5_GROUP, :LANES]), 0.0).astype(BF16)
                for lg in range(t)]
        zero = jnp.zeros((LANES, LANES), BF16)
        for tp in range(t):
            for s in range((tp // qt + 1) * qt):
                blk = lags[tp - s] if tp >= s else zero
                wt_ref[s * LANES:(s + 1) * LANES, tp * LANES:(tp + 1) * LANES] = blk
        kc = wc_ref.shape[0]
        row_group = lax.broadcasted_iota(jnp.int32, (kc, qw), 0) // (2 * S5_STATE)
        col_group = (lax.broadcasted_iota(jnp.int32, (kc, qw), 1) % LANES) // S5_GROUP
        for q in range(nq):
            cols = slice(q * qw, (q + 1) * qw)
            wide = _dot(wcc_ref[0], spread_ref[:, cols])
            wc_ref[:, cols] = jnp.where(row_group == col_group, wide, 0.0).astype(BF16)

    a = _s5_rows(h_ref, nc)
    xp = jnp.concatenate([x_ref[:, g, :] for g in range(S5_GPB)], axis=1).astype(BF16)
    for q in range(nq):
        cols = slice(q * qw, (q + 1) * qw)
        kq = (q + 1) * qw
        y = _dot(a[:, :kq], wt_ref[:kq, cols]) + _dot(xp, wc_ref[:, cols])
        for tt in range(qt):
            o_ref[pl.ds(q * qt + tt, nc, stride=t), :] = y[:, tt * LANES:(tt + 1) * LANES]


def _s5_out(h, xprev, w_lag, w_carry, batch, seq):
    nc = seq // S5_T
    nblk = w_lag.shape[0]
    kc = w_carry.shape[1]
    k = S5_T * LANES
    return pl.pallas_call(
        functools.partial(_s5_out_kernel, nc=nc),
        grid=(nblk, batch),
        in_specs=[pl.BlockSpec((seq, LANES), lambda j, b: (b, j)),
                  pl.BlockSpec((nc, S5_GPB, LANES), lambda j, b: (b, j, 0)),
                  pl.BlockSpec((1,) + w_lag.shape[1:], lambda j, b: (j, 0, 0, 0)),
                  pl.BlockSpec((1, kc, S5_ROW), lambda j, b: (j, 0, 0)),
                  _resident((S5_ROW, k))],
        out_specs=pl.BlockSpec((seq, LANES), lambda j, b: (b, j)),
        out_shape=jax.ShapeDtypeStruct(h.shape, F32),
        scratch_shapes=[pltpu.VMEM((k, k), BF16), pltpu.VMEM((kc, k), BF16)],
        compiler_params=_cparams("parallel", "arbitrary"),
        name="s5_chunk_out",
    )(h, xprev, w_lag, w_carry, _s5_spread())


def _s5_tables(a_re, a_im, log_step, b_re, b_im, c_re, c_im):
    hp = lax.Precision.HIGHEST
    t = S5_T
    step = jnp.exp(log_step)[:, None]
    mag = jnp.exp(step * a_re)
    ab_r = mag * jnp.cos(step * a_im)
    ab_i = mag * jnp.sin(step * a_im)
    den = a_re * a_re + a_im * a_im
    coef_r = ((ab_r - 1.0) * a_re + ab_i * a_im) / den
    coef_i = (ab_i * a_re - (ab_r - 1.0) * a_im) / den
    bb_r = coef_r[..., None] * b_re - coef_i[..., None] * b_im
    bb_i = coef_r[..., None] * b_im + coef_i[..., None] * b_re
    j = jnp.arange(t + 1, dtype=F32)[:, None, None]
    pmag = jnp.exp(j * (step * a_re)[None])
    pw_r = pmag * jnp.cos(j * (step * a_im)[None])
    pw_i = pmag * jnp.sin(j * (step * a_im)[None])

    rev_r, rev_i = pw_r[t - 1::-1], pw_i[t - 1::-1]
    ws_r = rev_r[:, :, :, None] * bb_r[None] - rev_i[:, :, :, None] * bb_i[None]
    ws_i = rev_r[:, :, :, None] * bb_i[None] + rev_i[:, :, :, None] * bb_r[None]
    w_sum = jnp.concatenate([ws_r, ws_i], axis=2)
    w_sum = w_sum.transpose(1, 0, 3, 2).reshape(S5_GROUPS, S5_ROW, 2 * S5_STATE)

    lb_r = pw_r[:t, :, :, None] * bb_r[None] - pw_i[:t, :, :, None] * bb_i[None]
    lb_i = pw_r[:t, :, :, None] * bb_i[None] + pw_i[:t, :, :, None] * bb_r[None]
    kmat = (jnp.einsum('gdp,jgpc->jgdc', c_re, lb_r, precision=hp)
            - jnp.einsum('gdp,jgpc->jgdc', c_im, lb_i, precision=hp))

    q_r, q_i = pw_r[1:], pw_i[1:]
    wc_r = c_re[None] * q_r[:, :, None, :] - c_im[None] * q_i[:, :, None, :]
    wc_i = -(c_re[None] * q_i[:, :, None, :] + c_im[None] * q_r[:, :, None, :])
    w_carry = jnp.concatenate([wc_r, wc_i], axis=3)
    w_carry = w_carry.transpose(1, 3, 0, 2).reshape(S5_GROUPS, 2 * S5_STATE, S5_ROW)

    a1 = jnp.concatenate([pw_r[t], pw_r[t]], axis=1)
    a2 = jnp.concatenate([-pw_i[t], pw_i[t]], axis=1)

    gl = S5_GPB
    nblk = S5_GROUPS // gl
    ws = w_sum.astype(BF16).reshape(nblk, gl, t, S5_GROUP, 2 * S5_STATE).transpose(0, 2, 1, 3, 4)
    ws = ws.reshape(nblk, t * LANES, 2 * S5_STATE)

    lag = kmat.transpose(0, 1, 3, 2).reshape(t, nblk, LANES, S5_GROUP).astype(BF16)
    lag = lag.transpose(1, 0, 2, 3)
    wc = w_carry.astype(BF16).reshape(nblk, gl * 2 * S5_STATE, S5_ROW)
    return ws, lag, wc, a1, a2


def _s5_ssm(h, tables, batch, seq):
    w_sum, w_lag, w_carry, a1, a2 = tables
    assert seq % S5_T == 0 and h.shape == (batch * seq, D_MODEL)
    nc = seq // S5_T
    s = _s5_sum(h, w_sum, batch, seq)
    s4 = s.reshape(batch, nc, S5_GROUPS, 2 * S5_STATE)
    xprev = _s5_scan(s4, a1, a2).reshape(s.shape)
    return _s5_out(h, xprev, w_lag, w_carry, batch, seq)


def kernel(x, norm_g, ffn_w_in, ffn_w_out, mix_w_in, attn_sinks, hgrn_lb, mix_w_out, s5_a_re, s5_a_im, s5_log_step, s5_b_re, s5_b_im, s5_c_re, s5_c_im, s5_d, s5_w_glu, final_g):
    batch, seq, d = x.shape
    n = batch * seq
    x = x.reshape(n, d)

    w_in16 = ffn_w_in.astype(BF16)
    w_out16 = ffn_w_out.astype(BF16)
    mix_in16 = mix_w_in.astype(BF16)
    mix_out16 = mix_w_out.astype(BF16)
    glu16 = s5_w_glu.astype(BF16)
    fg = final_g.reshape(1, d)

    lb_p = jax.nn.softmax(hgrn_lb.astype(F32), axis=0)
    lb_all = jnp.cumsum(lb_p, axis=0) - lb_p[0]

    for layer in range(DEPTH):
        g3 = norm_g[layer].reshape(3, 1, d)
        if layer % 2 == 0:
            e = layer // 2
            x = _ffn(x, g3[0], w_in16, w_out16, layer, 0, g3[1])
            proj = _norm_proj(x, g3[1], mix_in16[e])
            attn = _attention(proj, attn_sinks[e], batch, seq)
            rec = _hgrn(proj, lb_all[e], batch, seq)
            pre, pre_args = "mix", (attn, rec, mix_out16, e)
        else:
            o = layer // 2
            tables = _s5_tables(s5_a_re[o], s5_a_im[o], s5_log_step[o], s5_b_re[o], s5_b_im[o],
                                s5_c_re[o], s5_c_im[o])
            x, h = _ffn(x, g3[0], w_in16, w_out16, layer, 0, g3[1], post="also_norm")
            y = _s5_ssm(h, tables, batch, seq)
            pre, pre_args = "s5", (h, y, s5_d[o].reshape(1, d), glu16, o)
        last = layer == DEPTH - 1
        x = _ffn(x, g3[2], w_in16, w_out16, layer, 1, fg, pre=pre, pre_args=pre_args,
                 post="norm_only" if last else "none")
    return x.reshape(batch, seq, d)
```
